```python
import jax, jax.numpy as jnp
from jax import lax
import numpy as np

D_MODEL = 2048
BATCH = 4
SEQ = 4096
DEPTH = 1

HEAD_DIM = 128
N_HEADS_A = D_MODEL // (2 * HEAD_DIM)
N_KV_A = 2
N_HEADS_B = D_MODEL // (2 * HEAD_DIM)
N_KV_B = 2
N_HEADS = N_HEADS_A + N_HEADS_B
MIX_WIDTH = N_HEADS * HEAD_DIM
IDX_HEADS = 16
IDX_DIM = 64
TOPK_MAX = 256
WINDOW = 128
Q_BLOCK = 128
D_FF = 11 * D_MODEL // 4
CONV_WIDTH = 3
N_MOD = 6
EPS = 1e-6
NEG = -1e30
IN_WIDTH = (N_HEADS_A * HEAD_DIM + 2 * N_KV_A * HEAD_DIM + IDX_HEADS * IDX_DIM + IDX_DIM + IDX_HEADS
            + N_HEADS_B * HEAD_DIM + 2 * N_KV_B * HEAD_DIM)

kernel_name = "hymba_dsa_swa_sink_alibi_convffn_adaln"


def rms_norm(x, g):
    xf = x.astype(jnp.float32)
    y = xf * lax.rsqrt(jnp.mean(xf * xf, axis=-1, keepdims=True) + EPS)
    return (y * g.astype(jnp.float32)).astype(x.dtype)


def alibi_slopes(n):
    return 2.0 ** (-8.0 * jnp.arange(1, n + 1, dtype=jnp.float32) / n)


def split_sizes():
    return [N_HEADS_A * HEAD_DIM, N_KV_A * HEAD_DIM, N_KV_A * HEAD_DIM,
            IDX_HEADS * IDX_DIM, IDX_DIM, IDX_HEADS,
            N_HEADS_B * HEAD_DIM, N_KV_B * HEAD_DIM, N_KV_B * HEAD_DIM]


def dsa_attention(q, k, v, q_idx, k_idx, w_idx, slopes):
    B, S, H, dh = q.shape
    G = k.shape[2]
    R = H // G
    top_k = min(TOPK_MAX, S // 4)
    nb = S // Q_BLOCK
    key_pos = jnp.arange(S)
    scale = dh ** -0.5
    idx_scale = IDX_DIM ** -0.5
    slopes_gr = slopes.reshape(G, R)
    gather = jax.vmap(lambda a, i: a[i])

    def to_blocks(a):
        return jnp.moveaxis(a.reshape((B, nb, Q_BLOCK) + a.shape[2:]), 1, 0)

    def block(args):
        qb, qib, wb, start = args
        t = start + jnp.arange(Q_BLOCK)
        s_h = jnp.einsum('bqhd,bkd->bqhk', qib, k_idx, preferred_element_type=jnp.float32)
        score = jnp.einsum('bqhk,bqh->bqk', jax.nn.relu(s_h * idx_scale), wb.astype(jnp.float32))
        causal = key_pos[None, :] <= t[:, None]
        score = jnp.where(causal[None], score, NEG)
        _, sel = lax.top_k(score, top_k)
        valid = sel <= t[None, :, None]
        ks = gather(k, sel)
        vs = gather(v, sel)
        qg = qb.reshape(B, Q_BLOCK, G, R, dh)
        logits = jnp.einsum('bqgrd,bqkgd->bqgrk', qg, ks, preferred_element_type=jnp.float32) * scale
        dist = (t[None, :, None] - sel).astype(jnp.float32)
        logits = logits - slopes_gr[None, None, :, :, None] * dist[:, :, None, None, :]
        logits = jnp.where(valid[:, :, None, None, :], logits, NEG)
        p = jax.nn.softmax(logits, axis=-1).astype(v.dtype)
        o = jnp.einsum('bqgrk,bqkgd->bqgrd', p, vs)
        return o.reshape(B, Q_BLOCK, H * dh)

    starts = jnp.arange(nb, dtype=jnp.int32) * Q_BLOCK
    out = lax.map(block, (to_blocks(q), to_blocks(q_idx), to_blocks(w_idx), starts))
    return jnp.moveaxis(out, 0, 1).reshape(B, S, H * dh)


def swa_sink_attention(q, k, v, sinks, slopes):
    B, S, H, dh = q.shape
    G = k.shape[2]
    R = H // G
    nb = S // WINDOW
    qb = q.reshape(B, nb, WINDOW, G, R, dh)

    def band(a):
        a = a.reshape(B, nb, WINDOW, G, dh)
        prev = jnp.pad(a, ((0, 0), (1, 0), (0, 0), (0, 0), (0, 0)))[:, :-1]
        return jnp.concatenate([prev, a], axis=2)

    kb, vb = band(k), band(v)
    logits = jnp.einsum('bnqgrd,bnkgd->bngrqk', qb, kb, preferred_element_type=jnp.float32) * (dh ** -0.5)
    i = jnp.arange(WINDOW)[:, None]
    j = jnp.arange(2 * WINDOW)[None, :]
    dist = i + WINDOW - j
    key_abs = jnp.arange(nb)[:, None, None] * WINDOW - WINDOW + j[None]
    valid = (dist >= 0) & (dist < WINDOW) & (key_abs >= 0)
    logits = logits - slopes.reshape(G, R)[None, None, :, :, None, None] * dist.astype(jnp.float32)
    logits = jnp.where(valid[None, :, None, None], logits, NEG)
    sink = jnp.broadcast_to(sinks.astype(jnp.float32).reshape(1, 1, G, R, 1, 1), logits.shape[:-1] + (1,))
    p = jax.nn.softmax(jnp.concatenate([logits, sink], axis=-1), axis=-1)[..., :-1].astype(v.dtype)
    o = jnp.einsum('bngrqk,bnkgd->bnqgrd', p, vb)
    return o.reshape(B, S, H * dh)


def conv_gated_mlp(h, w_gate, w_up, conv_w, conv_b, w_down):
    g = h @ w_gate
    u = h @ w_up
    F = g.shape[-1]
    g = lax.conv_general_dilated(g, conv_w[:, None, :], window_strides=(1,),
                                 padding=[(CONV_WIDTH - 1, 0)],
                                 dimension_numbers=('NWC', 'WIO', 'NWC'),
                                 feature_group_count=F) + conv_b
    return (jax.nn.silu(g) * u) @ w_down


def setup_inputs(seed: int = 0) -> dict:
    key = jax.random.key(seed)
    ks = jax.random.split(key, 16)
    f32 = jnp.float32
    D = D_MODEL
    nrm = lambda k, shape, s: jax.random.normal(k, shape, f32) * s
    return {
        "x": nrm(ks[0], (BATCH, SEQ, D), 1.0),
        "c": nrm(ks[1], (BATCH, D), 1.0),
        "w_ada": nrm(ks[2], (DEPTH, D, N_MOD * D), 0.5 * D ** -0.5),
        "b_ada": nrm(ks[3], (DEPTH, N_MOD * D), 0.02),
        "g_mix": 1.0 + nrm(ks[4], (DEPTH, D), 0.02),
        "w_in": nrm(ks[5], (DEPTH, D, IN_WIDTH), D ** -0.5),
        "sinks": nrm(ks[6], (DEPTH, N_HEADS_B), 1.0),
        "w_o": nrm(ks[7], (DEPTH, MIX_WIDTH, D), MIX_WIDTH ** -0.5),
        "g_ffn": 1.0 + nrm(ks[8], (DEPTH, D), 0.02),
        "w_gate": nrm(ks[9], (DEPTH, D, D_FF), D ** -0.5),
        "w_up": nrm(ks[10], (DEPTH, D, D_FF), D ** -0.5),
        "conv_w": nrm(ks[11], (DEPTH, CONV_WIDTH, D_FF), CONV_WIDTH ** -0.5),
        "conv_b": nrm(ks[12], (DEPTH, D_FF), 0.02),
        "w_down": nrm(ks[13], (DEPTH, D_FF, D), D_FF ** -0.5),
        "g_final": 1.0 + nrm(ks[14], (D,), 0.02),
    }


def reference(x, c, w_ada, b_ada, g_mix, w_in, sinks, w_o, g_ffn, w_gate, w_up, conv_w, conv_b, w_down, g_final):
    B, S, _ = x.shape
    slopes = alibi_slopes(N_HEADS)
    slopes_a, slopes_b = slopes[0::2], slopes[1::2]
    sizes = split_sizes()
    split_points = [sum(sizes[:n]) for n in range(1, len(sizes))]
    for layer in range(DEPTH):
        mod = jax.nn.silu(c) @ w_ada[layer] + b_ada[layer]
        shift1, scale1, gate1, shift2, scale2, gate2 = jnp.split(mod[:, None, :], N_MOD, axis=-1)
        h = rms_norm(x, g_mix[layer]) * (1 + scale1) + shift1
        proj = h @ w_in[layer]
        qa, ka, va, qi, ki, wi, qb, kb, vb = jnp.split(proj, split_points, axis=-1)
        oa = dsa_attention(qa.reshape(B, S, N_HEADS_A, HEAD_DIM),
                           ka.reshape(B, S, N_KV_A, HEAD_DIM),
                           va.reshape(B, S, N_KV_A, HEAD_DIM),
                           qi.reshape(B, S, IDX_HEADS, IDX_DIM),
                           ki,
                           wi * (IDX_HEADS ** -0.5),
                           slopes_a)
        ob = swa_sink_attention(qb.reshape(B, S, N_HEADS_B, HEAD_DIM),
                                kb.reshape(B, S, N_KV_B, HEAD_DIM),
                                vb.reshape(B, S, N_KV_B, HEAD_DIM),
                                sinks[layer], slopes_b)
        mix = jnp.concatenate([oa, ob], axis=-1) @ w_o[layer]
        x = x + gate1 * mix
        h2 = rms_norm(x, g_ffn[layer]) * (1 + scale2) + shift2
        x = x + gate2 * conv_gated_mlp(h2, w_gate[layer], w_up[layer], conv_w[layer], conv_b[layer], w_down[layer])
    return rms_norm(x, g_final)
```

```python
import functools

import jax
import jax.numpy as jnp
from jax import lax
from jax.experimental import pallas as pl
from jax.experimental.pallas import tpu as pltpu

F32 = jnp.float32
BF16 = jnp.bfloat16

D_MODEL = 2048
HEAD_DIM = 128
N_HEADS_A = 8
N_HEADS_B = 8
N_KV = 2
HEADS_PER_KV = 4
IDX_HEADS = 16
IDX_DIM = 64
TOPK_MAX = 256
WINDOW = 128
D_FF = 5632
N_MOD = 6
EPS = 1e-6
NEG = -1e30
ATTN_SCALE = HEAD_DIM ** -0.5
IDX_W_SCALE = (IDX_DIM ** -0.5) * (IDX_HEADS ** -0.5)
SLOPES = [2.0 ** (-8.0 * j / 16.0) for j in range(1, 17)]
SLOPES_A = SLOPES[0::2]
SLOPES_B = SLOPES[1::2]

COL_QA, COL_QI, COL_QB = 0, 1024, 2048
COL_KA, COL_VA, COL_KB, COL_VB = 3072, 3328, 3584, 3840
COL_KIA, COL_KIB = 4096, 4224
PROJ_W = 4352

VMEM_LIMIT = 56 * 1024 * 1024

KEY_LO0 = -2139095040
KEY_HI0 = 2139095040


def _cparams(sem):
    return pltpu.CompilerParams(dimension_semantics=sem, vmem_limit_bytes=VMEM_LIMIT)


def _ada_kernel(c_ref, w_ref, b_ref, o_ref):
    c = c_ref[...]
    s = c / (1.0 + jnp.exp(-c))
    o_ref[...] = jnp.dot(s.astype(BF16), w_ref[...].astype(BF16),
                         preferred_element_type=F32) + b_ref[...]


def _ada_mod(c, w_ada, b_ada):
    B, D = c.shape
    N = w_ada.shape[1]
    tn = 1024
    cp = jnp.zeros((8, D), F32).at[:B].set(c)
    out = pl.pallas_call(
        _ada_kernel,
        grid=(N // tn,),
        in_specs=[pl.BlockSpec((8, D), lambda j: (0, 0)),
                  pl.BlockSpec((D, tn), lambda j: (0, j)),
                  pl.BlockSpec((1, tn), lambda j: (0, j))],
        out_specs=pl.BlockSpec((8, tn), lambda j: (0, j)),
        out_shape=jax.ShapeDtypeStruct((8, N), F32),
        compiler_params=_cparams(("arbitrary",)),
        name="ada_mod",
    )(cp, w_ada, b_ada.reshape(1, N))
    return out[:B].reshape(B, N_MOD, D)


def _inproj_kernel(x_ref, mod_ref, g_ref, w_ref, wt_ref, p_ref, t_ref, *, chunk):
    x = x_ref[0]
    ms = jnp.mean(x * x, axis=-1, keepdims=True)
    y = x * lax.rsqrt(ms + EPS) * g_ref[...]
    h = y * (1.0 + mod_ref[0, 1:2, :]) + mod_ref[0, 0:1, :]
    hb = h.astype(BF16)
    for n in range(0, PROJ_W, chunk):
        p_ref[0, :, n:n + chunk] = jnp.dot(
            hb, w_ref[:, n:n + chunk], preferred_element_type=F32).astype(BF16)
    t_ref[0] = jnp.dot(hb, wt_ref[...], preferred_element_type=F32)


def _in_proj(x, mod, g_mix, w_big, w_tail):
    B, S, D = x.shape
    tm = 512
    return pl.pallas_call(
        functools.partial(_inproj_kernel, chunk=256),
        grid=(B, S // tm),
        in_specs=[pl.BlockSpec((1, tm, D), lambda b, i: (b, i, 0)),
                  pl.BlockSpec((1, N_MOD, D), lambda b, i: (b, 0, 0)),
                  pl.BlockSpec((1, D), lambda b, i: (0, 0)),
                  pl.BlockSpec((D, PROJ_W), lambda b, i: (0, 0)),
                  pl.BlockSpec((D, 128), lambda b, i: (0, 0))],
        out_specs=[pl.BlockSpec((1, tm, PROJ_W), lambda b, i: (b, i, 0)),
                   pl.BlockSpec((1, tm, 128), lambda b, i: (b, i, 0))],
        out_shape=[jax.ShapeDtypeStruct((B, S, PROJ_W), BF16),
                   jax.ShapeDtypeStruct((B, S, 128), F32)],
        compiler_params=_cparams(("arbitrary", "arbitrary")),
        name="in_proj",
    )(x, mod, g_mix.reshape(1, D), w_big, w_tail)


def _key_to_f32(k):
    bits = k ^ ((k >> 31) & 0x7FFFFFFF)
    return lax.bitcast_convert_type(bits, F32)


def _dsa_kernel(qa_ref, qi_ref, w_ref, ka_ref, va_ref, kia_ref, kib_ref, o_ref,
                score_ref, wb_ref, qst_ref, p_ref, thr_ref, m_ref, l_ref, acc_ref,
                *, TQ, TK, S, RG):
    i = pl.program_id(1)
    q0 = i * TQ
    n_ch = i + 1
    n_rg = TQ // RG
    kf = float(TOPK_MAX)
    dn_nt = (((1,), (1,)), ((), ()))

    row_i = lax.broadcasted_iota(jnp.int32, (TQ, TK), 0)
    col_i = lax.broadcasted_iota(jnp.int32, (TQ, TK), 1)
    col_g = lax.broadcasted_iota(jnp.int32, (RG, TK), 1)

    wsc = w_ref[0] * IDX_W_SCALE
    for h in range(IDX_HEADS):
        wb_ref[h] = jnp.broadcast_to(wsc[:, h:h + 1], (TQ, TK))
    for g in range(N_KV):
        for r in range(HEADS_PER_KV):
            hh = g * HEADS_PER_KV + r
            qst_ref[g, r * TQ:(r + 1) * TQ, :] = qa_ref[0, :, hh * HEAD_DIM:(hh + 1) * HEAD_DIM]

    def idx_chunk(kc, carry):
        k0 = pl.multiple_of(kc * TK, TK)
        kia = kia_ref[0, pl.ds(k0, TK), :]
        kib = kib_ref[0, pl.ds(k0, TK), :]
        acc = jnp.zeros((TQ, TK), F32)
        for j in range(IDX_HEADS // 2):
            qp = qi_ref[0, :, j * 128:(j + 1) * 128]
            sa = lax.dot_general(qp, kia, dn_nt, preferred_element_type=F32)
            sb = lax.dot_general(qp, kib, dn_nt, preferred_element_type=F32)
            acc = acc + jnp.maximum(sa, 0.0) * wb_ref[2 * j] + jnp.maximum(sb, 0.0) * wb_ref[2 * j + 1]
        causal = (k0 + col_i) <= (q0 + row_i)
        score_ref[kc] = jnp.where(causal, acc, -jnp.inf)
        return carry

    lax.fori_loop(0, n_ch, idx_chunk, 0)

    def count_rows(rg, pred):
        def body(kc, acc):
            m = pred(score_ref[kc, rg * RG:(rg + 1) * RG, :], kc)
            return acc + m[:, :128] + m[:, 128:]

        acc = lax.fori_loop(0, n_ch, body, jnp.zeros((RG, 128), F32))
        return jnp.sum(acc, axis=1, keepdims=True)

    def count_ge(rg, t):
        return count_rows(rg, lambda blk, kc: jnp.where(blk >= t, 1.0, 0.0))

    t_idx = [q0 + rg * RG + lax.broadcasted_iota(jnp.int32, (RG, 1), 0) for rg in range(n_rg)]
    n_masked = [(S - 1 - t).astype(F32) for t in t_idx]
    n_causal = [(t + 1).astype(F32) for t in t_idx]

    def bis_cond(st):
        return jnp.logical_and(st[4] > 0.5, st[5] < 40)

    def bis_body(st):
        lo, hi, cnt_lo, cph_lo, _, it = st
        lo, hi, cnt_lo, cph_lo = list(lo), list(hi), list(cnt_lo), list(cph_lo)
        nd = jnp.float32(0.0)
        for rg in range(n_rg):
            mid = (lo[rg] >> 1) + (hi[rg] >> 1) + (lo[rg] & hi[rg] & 1)
            fm = _key_to_f32(mid)
            cph = count_ge(rg, fm)
            c = cph + jnp.where(fm <= NEG, n_masked[rg], 0.0)
            ge = c >= kf
            lo[rg] = jnp.where(ge, mid, lo[rg])
            hi[rg] = jnp.where(ge, hi[rg], mid)
            cnt_lo[rg] = jnp.where(ge, c, cnt_lo[rg])
            cph_lo[rg] = jnp.where(ge, cph, cph_lo[rg])
            done = jnp.logical_or(cnt_lo[rg] == kf, lo[rg] + 1 >= hi[rg])
            nd = jnp.maximum(nd, jnp.max(jnp.where(done, 0.0, 1.0)))
        return tuple(lo), tuple(hi), tuple(cnt_lo), tuple(cph_lo), nd, it + 1

    st0 = (tuple(jnp.full((RG, 1), KEY_LO0, jnp.int32) for _ in range(n_rg)),
           tuple(jnp.full((RG, 1), KEY_HI0, jnp.int32) for _ in range(n_rg)),
           tuple(jnp.full((RG, 1), float(S), F32) for _ in range(n_rg)),
           tuple(n_causal), jnp.float32(1.0), jnp.int32(0))
    lo, hi, cnt_lo, cph_lo, _, _ = lax.while_loop(bis_cond, bis_body, st0)

    for rg in range(n_rg):
        thr_g = _key_to_f32(lo[rg])
        thr_ref[rg * RG:(rg + 1) * RG, :] = thr_g
        f_hi = _key_to_f32(hi[rg])
        masked_hi = jnp.where(f_hi <= NEG, n_masked[rg], 0.0)
        need = jnp.logical_and(cnt_lo[rg] > kf, cph_lo[rg] > kf - masked_hi)
        any_need = jnp.max(jnp.where(need, 1.0, 0.0)) > 0.5

        @pl.when(any_need)
        def _fix_ties(rg=rg, thr_g=thr_g, f_hi=f_hi, masked_hi=masked_hi, need=need):
            slots = kf - count_ge(rg, f_hi) - masked_hi

            def jb(_, st):
                jlo, jhi = st
                jm = (jlo + jhi) >> 1
                ties = count_rows(rg, lambda blk, kc: jnp.where(
                    blk == thr_g, jnp.where(kc * TK + col_g <= jm, 1.0, 0.0), 0.0))
                ge = ties >= slots
                return jnp.where(ge, jlo, jm), jnp.where(ge, jm, jhi)

            n_iter = max(1, (S - 1).bit_length())
            _, jcut = lax.fori_loop(0, n_iter, jb, (jnp.full((RG, 1), -1, jnp.int32),
                                                     jnp.full((RG, 1), S - 1, jnp.int32)))
            jcut = jnp.where(need, jcut, S)

            def drop(kc, carry):
                blk = score_ref[kc, rg * RG:(rg + 1) * RG, :]
                surplus = jnp.where(blk == thr_g, jnp.where(kc * TK + col_g > jcut, 1.0, 0.0), 0.0)
                score_ref[kc, rg * RG:(rg + 1) * RG, :] = jnp.where(surplus > 0.5, -jnp.inf, blk)
                return carry

            lax.fori_loop(0, n_ch, drop, 0)

    thr = thr_ref[...]

    m_ref[...] = jnp.full(m_ref.shape, NEG, F32)
    l_ref[...] = jnp.zeros(l_ref.shape, F32)
    acc_ref[...] = jnp.zeros(acc_ref.shape, F32)
    dist0 = (row_i - col_i).astype(F32)

    def att_chunk(kc, carry):
        k0 = pl.multiple_of(kc * TK, TK)
        sel = score_ref[kc] >= thr
        dist = dist0 + (q0 - k0).astype(F32)
        for g in range(N_KV):
            kg = ka_ref[0, pl.ds(k0, TK), g * HEAD_DIM:(g + 1) * HEAD_DIM]
            vg = va_ref[0, pl.ds(k0, TK), g * HEAD_DIM:(g + 1) * HEAD_DIM]
            s_all = lax.dot_general(qst_ref[g], kg, dn_nt, preferred_element_type=F32)
            alphas = []
            for r in range(HEADS_PER_KV):
                hh = g * HEADS_PER_KV + r
                lg = s_all[r * TQ:(r + 1) * TQ] * ATTN_SCALE - SLOPES_A[hh] * dist
                lm = jnp.where(sel, lg, NEG)
                m_old = m_ref[hh]
                m_new = jnp.maximum(m_old, jnp.max(lm, axis=1, keepdims=True))
                alpha = jnp.exp(m_old - m_new)
                p = jnp.exp(lm - m_new)
                l_ref[hh] = alpha * l_ref[hh] + jnp.sum(p, axis=1, keepdims=True)
                m_ref[hh] = m_new
                p_ref[r * TQ:(r + 1) * TQ, :] = p.astype(BF16)
                alphas.append(alpha)
            pv = jnp.dot(p_ref[...], vg, preferred_element_type=F32)
            for r in range(HEADS_PER_KV):
                hh = g * HEADS_PER_KV + r
                acc_ref[hh] = alphas[r] * acc_ref[hh] + pv[r * TQ:(r + 1) * TQ]
        return carry

    lax.fori_loop(0, n_ch, att_chunk, 0)

    for hh in range(N_HEADS_A):
        o_ref[0, :, hh * HEAD_DIM:(hh + 1) * HEAD_DIM] = (acc_ref[hh] / l_ref[hh]).astype(BF16)


def _dsa_attention(proj, tail):
    B, S, _ = proj.shape
    TQ = TK = 256
    kern = functools.partial(_dsa_kernel, TQ=TQ, TK=TK, S=S, RG=64)
    return pl.pallas_call(
        kern,
        grid=(B, S // TQ),
        in_specs=[pl.BlockSpec((1, TQ, 1024), lambda b, i: (b, i, COL_QA // 1024)),
                  pl.BlockSpec((1, TQ, 1024), lambda b, i: (b, i, COL_QI // 1024)),
                  pl.BlockSpec((1, TQ, 128), lambda b, i: (b, i, 0)),
                  pl.BlockSpec((1, S, 256), lambda b, i: (b, 0, COL_KA // 256)),
                  pl.BlockSpec((1, S, 256), lambda b, i: (b, 0, COL_VA // 256)),
                  pl.BlockSpec((1, S, 128), lambda b, i: (b, 0, COL_KIA // 128)),
                  pl.BlockSpec((1, S, 128), lambda b, i: (b, 0, COL_KIB // 128))],
        out_specs=pl.BlockSpec((1, TQ, 1024), lambda b, i: (b, i, 0)),
        out_shape=jax.ShapeDtypeStruct((B, S, N_HEADS_A * HEAD_DIM), BF16),
        scratch_shapes=[pltpu.VMEM((S // TK, TQ, TK), F32),
                        pltpu.VMEM((IDX_HEADS, TQ, TK), F32),
                        pltpu.VMEM((N_KV, HEADS_PER_KV * TQ, HEAD_DIM), BF16),
                        pltpu.VMEM((HEADS_PER_KV * TQ, TK), BF16),
                        pltpu.VMEM((TQ, 1), F32),
                        pltpu.VMEM((N_HEADS_A, TQ, 1), F32),
                        pltpu.VMEM((N_HEADS_A, TQ, 1), F32),
                        pltpu.VMEM((N_HEADS_A, TQ, HEAD_DIM), F32)],
        compiler_params=_cparams(("arbitrary", "arbitrary")),
        name="dsa_attn",
    )(proj, proj, tail, proj, proj, proj, proj)


def _swa_kernel(sink_ref, q_ref, k_ref, v_ref, o_ref, qst_ref, p_ref, *, TQ, KW):
    i = pl.program_id(1)
    q0 = i * TQ
    start = pl.multiple_of(jnp.maximum(q0 - WINDOW, 0), WINDOW)
    dn_nt = (((1,), (1,)), ((), ()))
    row_i = lax.broadcasted_iota(jnp.int32, (TQ, KW), 0)
    col_i = lax.broadcasted_iota(jnp.int32, (TQ, KW), 1)
    dist_i = (q0 + row_i) - (start + col_i)
    valid = jnp.logical_and(dist_i >= 0, dist_i < WINDOW)
    dist = dist_i.astype(F32)
    for g in range(N_KV):
        for r in range(HEADS_PER_KV):
            hh = g * HEADS_PER_KV + r
            qst_ref[r * TQ:(r + 1) * TQ, :] = q_ref[0, :, hh * HEAD_DIM:(hh + 1) * HEAD_DIM]
        kg = k_ref[0, pl.ds(start, KW), g * HEAD_DIM:(g + 1) * HEAD_DIM]
        vg = v_ref[0, pl.ds(start, KW), g * HEAD_DIM:(g + 1) * HEAD_DIM]
        s_all = lax.dot_general(qst_ref[...], kg, dn_nt, preferred_element_type=F32)
        for r in range(HEADS_PER_KV):
            hh = g * HEADS_PER_KV + r
            sink = sink_ref[hh]
            lg = s_all[r * TQ:(r + 1) * TQ] * ATTN_SCALE - SLOPES_B[hh] * dist
            lm = jnp.where(valid, lg, NEG)
            m = jnp.maximum(jnp.max(lm, axis=1, keepdims=True), sink)
            e = jnp.exp(lm - m)
            den = jnp.sum(e, axis=1, keepdims=True) + jnp.exp(sink - m)
            p_ref[r * TQ:(r + 1) * TQ, :] = (e / den).astype(BF16)
        pv = jnp.dot(p_ref[...], vg, preferred_element_type=F32)
        for r in range(HEADS_PER_KV):
            hh = g * HEADS_PER_KV + r
            o_ref[0, :, hh * HEAD_DIM:(hh + 1) * HEAD_DIM] = pv[r * TQ:(r + 1) * TQ].astype(BF16)


def _swa_attention(proj, sinks):
    B, S, _ = proj.shape
    TQ = 256
    KW = TQ + WINDOW
    kern = functools.partial(_swa_kernel, TQ=TQ, KW=KW)
    return pl.pallas_call(
        kern,
        grid=(B, S // TQ),
        in_specs=[pl.BlockSpec(memory_space=pltpu.SMEM),
                  pl.BlockSpec((1, TQ, 1024), lambda b, i: (b, i, COL_QB // 1024)),
                  pl.BlockSpec((1, S, 256), lambda b, i: (b, 0, COL_KB // 256)),
                  pl.BlockSpec((1, S, 256), lambda b, i: (b, 0, COL_VB // 256))],
        out_specs=pl.BlockSpec((1, TQ, 1024), lambda b, i: (b, i, 0)),
        out_shape=jax.ShapeDtypeStruct((B, S, N_HEADS_B * HEAD_DIM), BF16),
        scratch_shapes=[pltpu.VMEM((HEADS_PER_KV * TQ, HEAD_DIM), BF16),
                        pltpu.VMEM((HEADS_PER_KV * TQ, KW), BF16)],
        compiler_params=_cparams(("arbitrary", "arbitrary")),
        name="swa_attn",
    )(sinks, proj, proj, proj)


def _outproj_kernel(oa_ref, ob_ref, x_ref, mod_ref, g_ref, w_ref, x1_ref, h2_ref):
    half = oa_ref.shape[2]
    mix = jnp.dot(oa_ref[0], w_ref[:half, :], preferred_element_type=F32)
    mix = mix + jnp.dot(ob_ref[0], w_ref[half:, :], preferred_element_type=F32)
    x1 = x_ref[0] + mod_ref[0, 2:3, :] * mix
    x1_ref[0] = x1
    ms = jnp.mean(x1 * x1, axis=-1, keepdims=True)
    y = x1 * lax.rsqrt(ms + EPS) * g_ref[...]
    h2_ref[0] = (y * (1.0 + mod_ref[0, 4:5, :]) + mod_ref[0, 3:4, :]).astype(BF16)


def _out_proj(oa, ob, x, mod, g_ffn, w_o):
    B, S, D = x.shape
    tm = 512
    half = oa.shape[2]
    return pl.pallas_call(
        _outproj_kernel,
        grid=(B, S // tm),
        in_specs=[pl.BlockSpec((1, tm, half), lambda b, i: (b, i, 0)),
                  pl.BlockSpec((1, tm, half), lambda b, i: (b, i, 0)),
                  pl.BlockSpec((1, tm, D), lambda b, i: (b, i, 0)),
                  pl.BlockSpec((1, N_MOD, D), lambda b, i: (b, 0, 0)),
                  pl.BlockSpec((1, D), lambda b, i: (0, 0)),
                  pl.BlockSpec((2 * half, D), lambda b, i: (0, 0))],
        out_specs=[pl.BlockSpec((1, tm, D), lambda b, i: (b, i, 0)),
                   pl.BlockSpec((1, tm, D), lambda b, i: (b, i, 0))],
        out_shape=[jax.ShapeDtypeStruct((B, S, D), F32),
                   jax.ShapeDtypeStruct((B, S, D), BF16)],
        compiler_params=_cparams(("arbitrary", "arbitrary")),
        name="out_proj",
    )(oa, ob, x, mod, g_ffn.reshape(1, D), w_o)


def _ffn_kernel(h2_ref, x1_ref, mod_ref, wg_ref, wu_ref, cw_ref, cb_ref, wd_ref, gf_ref, o_ref,
                acc_ref, carry_ref, *, tm):
    i = pl.program_id(1)
    j = pl.program_id(2)
    nj = pl.num_programs(2)
    h2 = h2_ref[0]
    g = jnp.dot(h2, wg_ref[...], preferred_element_type=F32)
    u = jnp.dot(h2, wu_ref[...], preferred_element_type=F32)

    prev = jnp.where(i > 0, carry_ref[j], 0.0)
    carry_ref[j] = g[tm - 8:, :]
    row = lax.broadcasted_iota(jnp.int32, g.shape, 0)
    g1 = jnp.where(row == 0, prev[7:8, :], pltpu.roll(g, 1, axis=0))
    g2 = jnp.where(row == 0, prev[6:7, :], jnp.where(row == 1, prev[7:8, :], pltpu.roll(g, 2, axis=0)))
    gc = cw_ref[0:1, :] * g2 + cw_ref[1:2, :] * g1 + cw_ref[2:3, :] * g + cb_ref[...]
    a = gc / (1.0 + jnp.exp(-gc)) * u
    part = jnp.dot(a.astype(BF16), wd_ref[...], preferred_element_type=F32)

    @pl.when(j == 0)
    def _():
        acc_ref[...] = part

    @pl.when(j > 0)
    def _():
        acc_ref[...] += part

    @pl.when(j == nj - 1)
    def _():
        x2 = x1_ref[0] + mod_ref[0, 5:6, :] * acc_ref[...]
        ms = jnp.mean(x2 * x2, axis=-1, keepdims=True)
        o_ref[0] = x2 * lax.rsqrt(ms + EPS) * gf_ref[...]


def _conv_ffn(h2, x1, mod, w_gate, w_up, conv_w, conv_b, w_down, g_final):
    B, S, D = x1.shape
    F = w_gate.shape[1]
    tm, tf = 512, 512
    kern = functools.partial(_ffn_kernel, tm=tm)
    return pl.pallas_call(
        kern,
        grid=(B, S // tm, F // tf),
        in_specs=[pl.BlockSpec((1, tm, D), lambda b, i, j: (b, i, 0)),
                  pl.BlockSpec((1, tm, D), lambda b, i, j: (b, i, 0)),
                  pl.BlockSpec((1, N_MOD, D), lambda b, i, j: (b, 0, 0)),
                  pl.BlockSpec((D, tf), lambda b, i, j: (0, j)),
                  pl.BlockSpec((D, tf), lambda b, i, j: (0, j)),
                  pl.BlockSpec((3, tf), lambda b, i, j: (0, j)),
                  pl.BlockSpec((1, tf), lambda b, i, j: (0, j)),
                  pl.BlockSpec((tf, D), lambda b, i, j: (j, 0)),
                  pl.BlockSpec((1, D), lambda b, i, j: (0, 0))],
        out_specs=pl.BlockSpec((1, tm, D), lambda b, i, j: (b, i, 0)),
        out_shape=jax.ShapeDtypeStruct((B, S, D), F32),
        scratch_shapes=[pltpu.VMEM((tm, D), F32),
                        pltpu.VMEM((F // tf, 8, tf), F32)],
        compiler_params=_cparams(("arbitrary", "arbitrary", "arbitrary")),
        name="conv_ffn",
    )(h2, x1, mod, w_gate, w_up, conv_w, conv_b.reshape(1, F), w_down, g_final.reshape(1, D))


def _regroup_w_in(w_in):
    sizes = [1024, 256, 256, 1024, 64, 16, 1024, 256, 256]
    offs = [0]
    for s in sizes:
        offs.append(offs[-1] + s)
    qa, ka, va, qi, ki, wi, qb, kb, vb = [w_in[:, offs[n]:offs[n + 1]] for n in range(9)]
    z64 = jnp.zeros_like(ki)
    big = jnp.concatenate([qa, qi, qb, ka, va, kb, vb, ki, z64, z64, ki], axis=1).astype(BF16)
    tail = jnp.concatenate([wi, jnp.zeros((w_in.shape[0], 128 - IDX_HEADS), w_in.dtype)], axis=1).astype(BF16)
    return big, tail


def kernel(x, c, w_ada, b_ada, g_mix, w_in, sinks, w_o, g_ffn, w_gate, w_up, conv_w, conv_b, w_down, g_final):
    assert w_ada.shape[0] == 1, "the final norm is fused into the (single) layer's FFN kernel"
    mod = _ada_mod(c, w_ada[0], b_ada[0])
    w_big, w_tail = _regroup_w_in(w_in[0])
    proj, tail = _in_proj(x, mod, g_mix[0], w_big, w_tail)
    oa = _dsa_attention(proj, tail)
    ob = _swa_attention(proj, sinks[0])
    x1, h2 = _out_proj(oa, ob, x, mod, g_ffn[0], w_o[0].astype(BF16))
    return _conv_ffn(h2, x1, mod, w_gate[0].astype(BF16), w_up[0].astype(BF16),
                     conv_w[0], conv_b[0], w_down[0].astype(BF16), g_final)
```

```python
import functools

import jax
import jax.numpy as jnp
from jax import lax
from jax.experimental import pallas as pl
from jax.experimental.pallas import tpu as pltpu

F32 = jnp.float32
BF16 = jnp.bfloat16

D_MODEL = 2048
HEAD_DIM = 128
N_HEADS_A = 8
N_HEADS_B = 8
N_KV = 2
HEADS_PER_KV = 4
IDX_HEADS = 16
IDX_DIM = 64
TOPK_MAX = 256
WINDOW = 128
D_FF = 5632
N_MOD = 6
EPS = 1e-6
NEG = -1e30
ATTN_SCALE = HEAD_DIM ** -0.5
IDX_W_SCALE = (IDX_DIM ** -0.5) * (IDX_HEADS ** -0.5)
SLOPES = [2.0 ** (-8.0 * j / 16.0) for j in range(1, 17)]
SLOPES_A = SLOPES[0::2]
SLOPES_B = SLOPES[1::2]

COL_QA, COL_QI, COL_QB = 0, 1024, 2048
COL_KA, COL_VA, COL_KB, COL_VB = 3072, 3328, 3584, 3840
COL_KIA, COL_KIB = 4096, 4224
PROJ_W = 4352

VMEM_LIMIT = 56 * 1024 * 1024

KEY_LO0 = -2139095040
KEY_HI0 = 2139095040


def _cparams(sem):
    return pltpu.CompilerParams(dimension_semantics=sem, vmem_limit_bytes=VMEM_LIMIT)


def _ada_kernel(c_ref, w_ref, b_ref, o_ref):
    c = c_ref[...]
    s = c / (1.0 + jnp.exp(-c))
    o_ref[...] = jnp.dot(s.astype(BF16), w_ref[...].astype(BF16),
                         preferred_element_type=F32) + b_ref[...]


def _ada_mod(c, w_ada, b_ada):
    B, D = c.shape
    N = w_ada.shape[1]
    tn = 1024
    cp = jnp.zeros((8, D), F32).at[:B].set(c)
    out = pl.pallas_call(
        _ada_kernel,
        grid=(N // tn,),
        in_specs=[pl.BlockSpec((8, D), lambda j: (0, 0)),
                  pl.BlockSpec((D, tn), lambda j: (0, j)),
                  pl.BlockSpec((1, tn), lambda j: (0, j))],
        out_specs=pl.BlockSpec((8, tn), lambda j: (0, j)),
        out_shape=jax.ShapeDtypeStruct((8, N), F32),
        compiler_params=_cparams(("arbitrary",)),
        name="ada_mod",
    )(cp, w_ada, b_ada.reshape(1, N))
    return out[:B].reshape(B, N_MOD, D)


def _inproj_kernel(x_ref, mod_ref, g_ref, w_ref, wt_ref, p_ref, t_ref, *, chunk):
    x = x_ref[0]
    ms = jnp.mean(x * x, axis=-1, keepdims=True)
    y = x * lax.rsqrt(ms + EPS) * g_ref[...]
    h = y * (1.0 + mod_ref[0, 1:2, :]) + mod_ref[0, 0:1, :]
    hb = h.astype(BF16)
    for n in range(0, PROJ_W, chunk):
        p_ref[0, :, n:n + chunk] = jnp.dot(
            hb, w_ref[:, n:n + chunk], preferred_element_type=F32).astype(BF16)
    t_ref[0] = jnp.dot(hb, wt_ref[...], preferred_element_type=F32)


def _in_proj(x, mod, g_mix, w_big, w_tail):
    B, S, D = x.shape
    tm = 512
    return pl.pallas_call(
        functools.partial(_inproj_kernel, chunk=256),
        grid=(B, S // tm),
        in_specs=[pl.BlockSpec((1, tm, D), lambda b, i: (b, i, 0)),
                  pl.BlockSpec((1, N_MOD, D), lambda b, i: (b, 0, 0)),
                  pl.BlockSpec((1, D), lambda b, i: (0, 0)),
                  pl.BlockSpec((D, PROJ_W), lambda b, i: (0, 0)),
                  pl.BlockSpec((D, 128), lambda b, i: (0, 0))],
        out_specs=[pl.BlockSpec((1, tm, PROJ_W), lambda b, i: (b, i, 0)),
                   pl.BlockSpec((1, tm, 128), lambda b, i: (b, i, 0))],
        out_shape=[jax.ShapeDtypeStruct((B, S, PROJ_W), BF16),
                   jax.ShapeDtypeStruct((B, S, 128), F32)],
        compiler_params=_cparams(("arbitrary", "arbitrary")),
        name="in_proj",
    )(x, mod, g_mix.reshape(1, D), w_big, w_tail)


def _key_to_f32(k):
    bits = k ^ ((k >> 31) & 0x7FFFFFFF)
    return lax.bitcast_convert_type(bits, F32)


def _dsa_kernel(qa_ref, qi_ref, w_ref, ka_ref, va_ref, kia_ref, kib_ref, o_ref,
                score_ref, wb_ref, qst_ref, s_ref, p_ref, pv_ref, mb_ref, dist_ref, dist0_ref,
                thr_ref, m_ref, l_ref, al_ref, acc_ref,
                *, TQ, TK, S, RG, RT, STEPS):
    i = pl.program_id(1)
    q0 = i * TQ
    n_ch = i + 1
    n_rg = TQ // RG
    n_rt = TQ // RT
    kf = float(TOPK_MAX)
    dn_nt = (((1,), (1,)), ((), ()))

    row_i = lax.broadcasted_iota(jnp.int32, (TQ, TK), 0)
    col_i = lax.broadcasted_iota(jnp.int32, (TQ, TK), 1)
    col_g = lax.broadcasted_iota(jnp.int32, (RG, TK), 1)

    dist0_ref[...] = (row_i - col_i).astype(F32)
    wsc = w_ref[0] * IDX_W_SCALE
    for h in range(IDX_HEADS):
        wb_ref[h] = jnp.broadcast_to(wsc[:, h:h + 1], (TQ, TK))
    for g in range(N_KV):
        for r in range(HEADS_PER_KV):
            hh = g * HEADS_PER_KV + r
            qst_ref[g, r * TQ:(r + 1) * TQ, :] = qa_ref[0, :, hh * HEAD_DIM:(hh + 1) * HEAD_DIM]

    def idx_chunk(kc, carry):
        k0 = pl.multiple_of(kc * TK, TK)
        kia = kia_ref[0, pl.ds(k0, TK), :]
        kib = kib_ref[0, pl.ds(k0, TK), :]
        acc = jnp.zeros((TQ, TK), F32)
        for j in range(IDX_HEADS // 2):
            qp = qi_ref[0, :, j * 128:(j + 1) * 128]
            sa = lax.dot_general(qp, kia, dn_nt, preferred_element_type=F32)
            sb = lax.dot_general(qp, kib, dn_nt, preferred_element_type=F32)
            acc = acc + jnp.maximum(sa, 0.0) * wb_ref[2 * j] + jnp.maximum(sb, 0.0) * wb_ref[2 * j + 1]
        causal = (k0 + col_i) <= (q0 + row_i)
        score_ref[kc] = jnp.where(causal, acc, -jnp.inf)
        return carry

    lax.fori_loop(0, n_ch, idx_chunk, 0)

    def count_rows(rg, pred):
        def body(kc, acc):
            m = pred(score_ref[kc, rg * RG:(rg + 1) * RG, :], kc)
            return acc + m[:, :128] + m[:, 128:]

        return lax.fori_loop(0, n_ch, body, jnp.zeros((RG, 128), F32))

    def count_ge(rg, t):
        return count_rows(rg, lambda blk, kc: jnp.where(blk >= t, 1.0, 0.0))

    t_idx = [q0 + rg * RG + lax.broadcasted_iota(jnp.int32, (RG, 1), 0) for rg in range(n_rg)]
    n_masked = [(S - 1 - t).astype(F32) for t in t_idx]
    n_causal = [(t + 1).astype(F32) for t in t_idx]

    def bis_cond(st):
        return jnp.logical_and(st[4] > 0.5, st[5] < 40)

    def bis_step(lo, hi, cnt_lo, cph_lo):
        mid = [(lo[rg] >> 1) + (hi[rg] >> 1) + (lo[rg] & hi[rg] & 1) for rg in range(n_rg)]
        fm = [_key_to_f32(m) for m in mid]
        part = [count_ge(rg, fm[rg]) for rg in range(n_rg)]
        ndv = None
        for rg in range(n_rg):
            cph = jnp.sum(part[rg], axis=1, keepdims=True)
            c = cph + jnp.where(fm[rg] <= NEG, n_masked[rg], 0.0)
            ge = c >= kf
            lo[rg] = jnp.where(ge, mid[rg], lo[rg])
            hi[rg] = jnp.where(ge, hi[rg], mid[rg])
            cnt_lo[rg] = jnp.where(ge, c, cnt_lo[rg])
            cph_lo[rg] = jnp.where(ge, cph, cph_lo[rg])
            done = jnp.logical_or(cnt_lo[rg] == kf, lo[rg] + 1 >= hi[rg])
            nd_g = jnp.where(done, 0.0, 1.0)
            ndv = nd_g if ndv is None else jnp.maximum(ndv, nd_g)
        return ndv

    def bis_body(st):
        lo, hi, cnt_lo, cph_lo, _, it = st
        lo, hi, cnt_lo, cph_lo = list(lo), list(hi), list(cnt_lo), list(cph_lo)
        for _ in range(STEPS):
            ndv = bis_step(lo, hi, cnt_lo, cph_lo)
        return tuple(lo), tuple(hi), tuple(cnt_lo), tuple(cph_lo), jnp.max(ndv), it + STEPS

    st0 = (tuple(jnp.full((RG, 1), KEY_LO0, jnp.int32) for _ in range(n_rg)),
           tuple(jnp.full((RG, 1), KEY_HI0, jnp.int32) for _ in range(n_rg)),
           tuple(jnp.full((RG, 1), float(S), F32) for _ in range(n_rg)),
           tuple(n_causal), jnp.float32(1.0), jnp.int32(0))
    lo, hi, cnt_lo, cph_lo, _, _ = lax.while_loop(bis_cond, bis_body, st0)

    for rg in range(n_rg):
        thr_g = _key_to_f32(lo[rg])
        thr_ref[rg * RG:(rg + 1) * RG, :] = jnp.broadcast_to(thr_g, (RG, 128))
        f_hi = _key_to_f32(hi[rg])
        masked_hi = jnp.where(f_hi <= NEG, n_masked[rg], 0.0)
        need = jnp.logical_and(cnt_lo[rg] > kf, cph_lo[rg] > kf - masked_hi)
        any_need = jnp.max(jnp.where(need, 1.0, 0.0)) > 0.5

        @pl.when(any_need)
        def _fix_ties(rg=rg, thr_g=thr_g, f_hi=f_hi, masked_hi=masked_hi, need=need):
            n_gt = jnp.sum(count_ge(rg, f_hi), axis=1, keepdims=True)
            slots = kf - n_gt - masked_hi

            def jb(_, st):
                jlo, jhi = st
                jm = (jlo + jhi) >> 1
                ties = count_rows(rg, lambda blk, kc: jnp.where(
                    blk == thr_g, jnp.where(kc * TK + col_g <= jm, 1.0, 0.0), 0.0))
                ge = jnp.sum(ties, axis=1, keepdims=True) >= slots
                return jnp.where(ge, jlo, jm), jnp.where(ge, jm, jhi)

            n_iter = max(1, (S - 1).bit_length())
            _, jcut = lax.fori_loop(0, n_iter, jb, (jnp.full((RG, 1), -1, jnp.int32),
                                                     jnp.full((RG, 1), S - 1, jnp.int32)))
            jcut = jnp.where(need, jcut, S)

            def drop(kc, carry):
                blk = score_ref[kc, rg * RG:(rg + 1) * RG, :]
                surplus = jnp.where(blk == thr_g, jnp.where(kc * TK + col_g > jcut, 1.0, 0.0), 0.0)
                score_ref[kc, rg * RG:(rg + 1) * RG, :] = jnp.where(surplus > 0.5, -jnp.inf, blk)
                return carry

            lax.fori_loop(0, n_ch, drop, 0)

    m_ref[...] = jnp.full(m_ref.shape, NEG, F32)
    l_ref[...] = jnp.zeros(l_ref.shape, F32)
    acc_ref[...] = jnp.zeros(acc_ref.shape, F32)

    def att_chunk(kc, carry):
        k0 = pl.multiple_of(kc * TK, TK)
        delta = (q0 - k0).astype(F32)
        for g in range(N_KV):
            kg = ka_ref[0, pl.ds(k0, TK), g * HEAD_DIM:(g + 1) * HEAD_DIM]
            s_ref[g] = lax.dot_general(qst_ref[g], kg, dn_nt, preferred_element_type=F32)
        for rt in range(n_rt):
            rows = slice(rt * RT, (rt + 1) * RT)
            sc = score_ref[kc, rows, :]
            t = thr_ref[rows, :]
            mb_ref[rows, :128] = jnp.where(sc[:, :128] >= t, 0.0, NEG)
            mb_ref[rows, 128:] = jnp.where(sc[:, 128:] >= t, 0.0, NEG)
            dist_ref[rows, :] = dist0_ref[rows, :] + delta
        for g in range(N_KV):
            for r in range(HEADS_PER_KV):
                hh = g * HEADS_PER_KV + r
                for rt in range(n_rt):
                    rows = slice(rt * RT, (rt + 1) * RT)
                    srows = slice(r * TQ + rt * RT, r * TQ + (rt + 1) * RT)
                    lm = s_ref[g, srows, :] * ATTN_SCALE - SLOPES_A[hh] * dist_ref[rows, :] + mb_ref[rows, :]
                    lm_l, lm_r = lm[:, :128], lm[:, 128:]
                    m_old = m_ref[hh, rows, :]
                    m_new = jnp.maximum(m_old, jnp.max(jnp.maximum(lm_l, lm_r), axis=1, keepdims=True))
                    alpha = jnp.exp(m_old - m_new)
                    p_l = jnp.exp(lm_l - m_new)
                    p_r = jnp.exp(lm_r - m_new)
                    l_ref[hh, rows, :] = alpha * l_ref[hh, rows, :] + jnp.sum(p_l + p_r, axis=1, keepdims=True)
                    m_ref[hh, rows, :] = m_new
                    al_ref[hh, rows, :] = alpha
                    p_ref[g, srows, :128] = p_l.astype(BF16)
                    p_ref[g, srows, 128:] = p_r.astype(BF16)
            vg = va_ref[0, pl.ds(k0, TK), g * HEAD_DIM:(g + 1) * HEAD_DIM]
            pv_ref[g] = jnp.dot(p_ref[g], vg, preferred_element_type=F32)
            for r in range(HEADS_PER_KV):
                hh = g * HEADS_PER_KV + r
                for rt in range(n_rt):
                    rows = slice(rt * RT, (rt + 1) * RT)
                    srows = slice(r * TQ + rt * RT, r * TQ + (rt + 1) * RT)
                    acc_ref[hh, rows, :] = al_ref[hh, rows, :] * acc_ref[hh, rows, :] + pv_ref[g, srows, :]
        return carry

    lax.fori_loop(0, n_ch, att_chunk, 0)

    for hh in range(N_HEADS_A):
        o_ref[0, :, hh * HEAD_DIM:(hh + 1) * HEAD_DIM] = (acc_ref[hh] / l_ref[hh]).astype(BF16)


def _dsa_attention(proj, tail):
    B, S, _ = proj.shape
    TQ = TK = 256
    kern = functools.partial(_dsa_kernel, TQ=TQ, TK=TK, S=S, RG=128, RT=32, STEPS=2)
    return pl.pallas_call(
        kern,
        grid=(B, S // TQ),
        in_specs=[pl.BlockSpec((1, TQ, 1024), lambda b, i: (b, i, COL_QA // 1024)),
                  pl.BlockSpec((1, TQ, 1024), lambda b, i: (b, i, COL_QI // 1024)),
                  pl.BlockSpec((1, TQ, 128), lambda b, i: (b, i, 0)),
                  pl.BlockSpec((1, S, 256), lambda b, i: (b, 0, COL_KA // 256)),
                  pl.BlockSpec((1, S, 256), lambda b, i: (b, 0, COL_VA // 256)),
                  pl.BlockSpec((1, S, 128), lambda b, i: (b, 0, COL_KIA // 128)),
                  pl.BlockSpec((1, S, 128), lambda b, i: (b, 0, COL_KIB // 128))],
        out_specs=pl.BlockSpec((1, TQ, 1024), lambda b, i: (b, i, 0)),
        out_shape=jax.ShapeDtypeStruct((B, S, N_HEADS_A * HEAD_DIM), BF16),
        scratch_shapes=[pltpu.VMEM((S // TK, TQ, TK), F32),
                        pltpu.VMEM((IDX_HEADS, TQ, TK), F32),
                        pltpu.VMEM((N_KV, HEADS_PER_KV * TQ, HEAD_DIM), BF16),
                        pltpu.VMEM((N_KV, HEADS_PER_KV * TQ, TK), F32),
                        pltpu.VMEM((N_KV, HEADS_PER_KV * TQ, TK), BF16),
                        pltpu.VMEM((N_KV, HEADS_PER_KV * TQ, HEAD_DIM), F32),
                        pltpu.VMEM((TQ, TK), F32),
                        pltpu.VMEM((TQ, TK), F32),
                        pltpu.VMEM((TQ, TK), F32),
                        pltpu.VMEM((TQ, 128), F32),
                        pltpu.VMEM((N_HEADS_A, TQ, 128), F32),
                        pltpu.VMEM((N_HEADS_A, TQ, 128), F32),
                        pltpu.VMEM((N_HEADS_A, TQ, 128), F32),
                        pltpu.VMEM((N_HEADS_A, TQ, HEAD_DIM), F32)],
        compiler_params=_cparams(("arbitrary", "arbitrary")),
        name="dsa_attn",
    )(proj, proj, tail, proj, proj, proj, proj)


def _swa_kernel(sink_ref, q_ref, k_ref, v_ref, o_ref, qst_ref, p_ref, *, TQ, KW):
    i = pl.program_id(1)
    q0 = i * TQ
    start = pl.multiple_of(jnp.maximum(q0 - WINDOW, 0), WINDOW)
    dn_nt = (((1,), (1,)), ((), ()))
    row_i = lax.broadcasted_iota(jnp.int32, (TQ, KW), 0)
    col_i = lax.broadcasted_iota(jnp.int32, (TQ, KW), 1)
    dist_i = (q0 + row_i) - (start + col_i)
    valid = jnp.logical_and(dist_i >= 0, dist_i < WINDOW)
    dist = dist_i.astype(F32)
    for g in range(N_KV):
        for r in range(HEADS_PER_KV):
            hh = g * HEADS_PER_KV + r
            qst_ref[r * TQ:(r + 1) * TQ, :] = q_ref[0, :, hh * HEAD_DIM:(hh + 1) * HEAD_DIM]
        kg = k_ref[0, pl.ds(start, KW), g * HEAD_DIM:(g + 1) * HEAD_DIM]
        vg = v_ref[0, pl.ds(start, KW), g * HEAD_DIM:(g + 1) * HEAD_DIM]
        s_all = lax.dot_general(qst_ref[...], kg, dn_nt, preferred_element_type=F32)
        for r in range(HEADS_PER_KV):
            hh = g * HEADS_PER_KV + r
            sink = sink_ref[hh]
            lg = s_all[r * TQ:(r + 1) * TQ] * ATTN_SCALE - SLOPES_B[hh] * dist
            lm = jnp.where(valid, lg, NEG)
            m = jnp.maximum(jnp.max(lm, axis=1, keepdims=True), sink)
            e = jnp.exp(lm - m)
            den = jnp.sum(e, axis=1, keepdims=True) + jnp.exp(sink - m)
            p_ref[r * TQ:(r + 1) * TQ, :] = (e / den).astype(BF16)
        pv = jnp.dot(p_ref[...], vg, preferred_element_type=F32)
        for r in range(HEADS_PER_KV):
            hh = g * HEADS_PER_KV + r
            o_ref[0, :, hh * HEAD_DIM:(hh + 1) * HEAD_DIM] = pv[r * TQ:(r + 1) * TQ].astype(BF16)


def _swa_attention(proj, sinks):
    B, S, _ = proj.shape
    TQ = 256
    KW = TQ + WINDOW
    kern = functools.partial(_swa_kernel, TQ=TQ, KW=KW)
    return pl.pallas_call(
        kern,
        grid=(B, S // TQ),
        in_specs=[pl.BlockSpec(memory_space=pltpu.SMEM),
                  pl.BlockSpec((1, TQ, 1024), lambda b, i: (b, i, COL_QB // 1024)),
                  pl.BlockSpec((1, S, 256), lambda b, i: (b, 0, COL_KB // 256)),
                  pl.BlockSpec((1, S, 256), lambda b, i: (b, 0, COL_VB // 256))],
        out_specs=pl.BlockSpec((1, TQ, 1024), lambda b, i: (b, i, 0)),
        out_shape=jax.ShapeDtypeStruct((B, S, N_HEADS_B * HEAD_DIM), BF16),
        scratch_shapes=[pltpu.VMEM((HEADS_PER_KV * TQ, HEAD_DIM), BF16),
                        pltpu.VMEM((HEADS_PER_KV * TQ, KW), BF16)],
        compiler_params=_cparams(("arbitrary", "arbitrary")),
        name="swa_attn",
    )(sinks, proj, proj, proj)


def _outproj_kernel(oa_ref, ob_ref, x_ref, mod_ref, g_ref, w_ref, x1_ref, h2_ref):
    half = oa_ref.shape[2]
    mix = jnp.dot(oa_ref[0], w_ref[:half, :], preferred_element_type=F32)
    mix = mix + jnp.dot(ob_ref[0], w_ref[half:, :], preferred_element_type=F32)
    x1 = x_ref[0] + mod_ref[0, 2:3, :] * mix
    x1_ref[0] = x1
    ms = jnp.mean(x1 * x1, axis=-1, keepdims=True)
    y = x1 * lax.rsqrt(ms + EPS) * g_ref[...]
    h2_ref[0] = (y * (1.0 + mod_ref[0, 4:5, :]) + mod_ref[0, 3:4, :]).astype(BF16)


def _out_proj(oa, ob, x, mod, g_ffn, w_o):
    B, S, D = x.shape
    tm = 512
    half = oa.shape[2]
    return pl.pallas_call(
        _outproj_kernel,
        grid=(B, S // tm),
        in_specs=[pl.BlockSpec((1, tm, half), lambda b, i: (b, i, 0)),
                  pl.BlockSpec((1, tm, half), lambda b, i: (b, i, 0)),
                  pl.BlockSpec((1, tm, D), lambda b, i: (b, i, 0)),
                  pl.BlockSpec((1, N_MOD, D), lambda b, i: (b, 0, 0)),
                  pl.BlockSpec((1, D), lambda b, i: (0, 0)),
                  pl.BlockSpec((2 * half, D), lambda b, i: (0, 0))],
        out_specs=[pl.BlockSpec((1, tm, D), lambda b, i: (b, i, 0)),
                   pl.BlockSpec((1, tm, D), lambda b, i: (b, i, 0))],
        out_shape=[jax.ShapeDtypeStruct((B, S, D), F32),
                   jax.ShapeDtypeStruct((B, S, D), BF16)],
        compiler_params=_cparams(("arbitrary", "arbitrary")),
        name="out_proj",
    )(oa, ob, x, mod, g_ffn.reshape(1, D), w_o)


def _ffn_kernel(h2_ref, x1_ref, mod_ref, wg_ref, wu_ref, cw_ref, cb_ref, wd_ref, gf_ref, o_ref,
                acc_ref, carry_ref, *, tm):
    i = pl.program_id(1)
    j = pl.program_id(2)
    nj = pl.num_programs(2)
    h2 = h2_ref[0]
    g = jnp.dot(h2, wg_ref[...], preferred_element_type=F32)
    u = jnp.dot(h2, wu_ref[...], preferred_element_type=F32)

    prev = jnp.where(i > 0, carry_ref[j], 0.0)
    carry_ref[j] = g[tm - 8:, :]
    row = lax.broadcasted_iota(jnp.int32, g.shape, 0)
    g1 = jnp.where(row == 0, prev[7:8, :], pltpu.roll(g, 1, axis=0))
    g2 = jnp.where(row == 0, prev[6:7, :], jnp.where(row == 1, prev[7:8, :], pltpu.roll(g, 2, axis=0)))
    gc = cw_ref[0:1, :] * g2 + cw_ref[1:2, :] * g1 + cw_ref[2:3, :] * g + cb_ref[...]
    a = gc / (1.0 + jnp.exp(-gc)) * u
    part = jnp.dot(a.astype(BF16), wd_ref[...], preferred_element_type=F32)

    @pl.when(j == 0)
    def _():
        acc_ref[...] = part

    @pl.when(j > 0)
    def _():
        acc_ref[...] += part

    @pl.when(j == nj - 1)
    def _():
        x2 = x1_ref[0] + mod_ref[0, 5:6, :] * acc_ref[...]
        ms = jnp.mean(x2 * x2, axis=-1, keepdims=True)
        o_ref[0] = x2 * lax.rsqrt(ms + EPS) * gf_ref[...]


def _conv_ffn(h2, x1, mod, w_gate, w_up, conv_w, conv_b, w_down, g_final):
    B, S, D = x1.shape
    F = w_gate.shape[1]
    tm, tf = 512, 512
    kern = functools.partial(_ffn_kernel, tm=tm)
    return pl.pallas_call(
        kern,
        grid=(B, S // tm, F // tf),
        in_specs=[pl.BlockSpec((1, tm, D), lambda b, i, j: (b, i, 0)),
                  pl.BlockSpec((1, tm, D), lambda b, i, j: (b, i, 0)),
                  pl.BlockSpec((1, N_MOD, D), lambda b, i, j: (b, 0, 0)),
                  pl.BlockSpec((D, tf), lambda b, i, j: (0, j)),
                  pl.BlockSpec((D, tf), lambda b, i, j: (0, j)),
                  pl.BlockSpec((3, tf), lambda b, i, j: (0, j)),
                  pl.BlockSpec((1, tf), lambda b, i, j: (0, j)),
                  pl.BlockSpec((tf, D), lambda b, i, j: (j, 0)),
                  pl.BlockSpec((1, D), lambda b, i, j: (0, 0))],
        out_specs=pl.BlockSpec((1, tm, D), lambda b, i, j: (b, i, 0)),
        out_shape=jax.ShapeDtypeStruct((B, S, D), F32),
        scratch_shapes=[pltpu.VMEM((tm, D), F32),
                        pltpu.VMEM((F // tf, 8, tf), F32)],
        compiler_params=_cparams(("arbitrary", "arbitrary", "arbitrary")),
        name="conv_ffn",
    )(h2, x1, mod, w_gate, w_up, conv_w, conv_b.reshape(1, F), w_down, g_final.reshape(1, D))


def _regroup_w_in(w_in):
    sizes = [1024, 256, 256, 1024, 64, 16, 1024, 256, 256]
    offs = [0]
    for s in sizes:
        offs.append(offs[-1] + s)
    qa, ka, va, qi, ki, wi, qb, kb, vb = [w_in[:, offs[n]:offs[n + 1]] for n in range(9)]
    z64 = jnp.zeros_like(ki)
    big = jnp.concatenate([qa, qi, qb, ka, va, kb, vb, ki, z64, z64, ki], axis=1).astype(BF16)
    tail = jnp.concatenate([wi, jnp.zeros((w_in.shape[0], 128 - IDX_HEADS), w_in.dtype)], axis=1).astype(BF16)
    return big, tail


def kernel(x, c, w_ada, b_ada, g_mix, w_in, sinks, w_o, g_ffn, w_gate, w_up, conv_w, conv_b, w_down, g_final):
    assert w_ada.shape[0] == 1, "the final norm is fused into the (single) layer's FFN kernel"
    mod = _ada_mod(c, w_ada[0], b_ada[0])
    w_big, w_tail = _regroup_w_in(w_in[0])
    proj, tail = _in_proj(x, mod, g_mix[0], w_big, w_tail)
    oa = _dsa_attention(proj, tail)
    ob = _swa_attention(proj, sinks[0])
    x1, h2 = _out_proj(oa, ob, x, mod, g_ffn[0], w_o[0].astype(BF16))
    return _conv_ffn(h2, x1, mod, w_gate[0].astype(BF16), w_up[0].astype(BF16),
                     conv_w[0], conv_b[0], w_down[0].astype(BF16), g_final)
```

```python
import functools

import jax
import jax.numpy as jnp
from jax import lax
from jax.experimental import pallas as pl
from jax.experimental.pallas import tpu as pltpu

F32 = jnp.float32
BF16 = jnp.bfloat16

D_MODEL = 2048
HEAD_DIM = 128
N_HEADS_A = 8
N_HEADS_B = 8
N_KV = 2
HEADS_PER_KV = 4
IDX_HEADS = 16
IDX_DIM = 64
TOPK_MAX = 256
WINDOW = 128
D_FF = 5632
N_MOD = 6
EPS = 1e-6
NEG = -1e30
ATTN_SCALE = HEAD_DIM ** -0.5
LOG2E = 1.4426950408889634
IDX_W_SCALE = (IDX_DIM ** -0.5) * (IDX_HEADS ** -0.5)
SLOPES = [2.0 ** (-8.0 * j / 16.0) for j in range(1, 17)]
SLOPES_A = SLOPES[0::2]
SLOPES_B = SLOPES[1::2]

COL_QA, COL_QI, COL_QB = 0, 1024, 2048
COL_KA, COL_VA, COL_KB, COL_VB = 3072, 3328, 3584, 3840
COL_KIA, COL_KIB = 4096, 4224
PROJ_W = 4352

VMEM_LIMIT = 56 * 1024 * 1024

KEY_LO0 = -2139095040
KEY_HI0 = 2139095040


def _cparams(sem):
    return pltpu.CompilerParams(dimension_semantics=sem, vmem_limit_bytes=VMEM_LIMIT)


def _ada_kernel(c_ref, w_ref, b_ref, o_ref):
    c = c_ref[...]
    s = c / (1.0 + jnp.exp(-c))
    o_ref[...] = jnp.dot(s.astype(BF16), w_ref[...].astype(BF16),
                         preferred_element_type=F32) + b_ref[...]


def _ada_mod(c, w_ada, b_ada):
    B, D = c.shape
    N = w_ada.shape[1]
    tn = 1024
    cp = jnp.zeros((8, D), F32).at[:B].set(c)
    out = pl.pallas_call(
        _ada_kernel,
        grid=(N // tn,),
        in_specs=[pl.BlockSpec((8, D), lambda j: (0, 0)),
                  pl.BlockSpec((D, tn), lambda j: (0, j)),
                  pl.BlockSpec((1, tn), lambda j: (0, j))],
        out_specs=pl.BlockSpec((8, tn), lambda j: (0, j)),
        out_shape=jax.ShapeDtypeStruct((8, N), F32),
        compiler_params=_cparams(("arbitrary",)),
        name="ada_mod",
    )(cp, w_ada, b_ada.reshape(1, N))
    return out[:B].reshape(B, N_MOD, D)


def _inproj_kernel(x_ref, mod_ref, g_ref, w_ref, wt_ref, p_ref, t_ref, *, chunk):
    x = x_ref[0]
    ms = jnp.mean(x * x, axis=-1, keepdims=True)
    y = x * lax.rsqrt(ms + EPS) * g_ref[...]
    h = y * (1.0 + mod_ref[0, 1:2, :]) + mod_ref[0, 0:1, :]
    hb = h.astype(BF16)
    for n in range(0, PROJ_W, chunk):
        p_ref[0, :, n:n + chunk] = jnp.dot(
            hb, w_ref[:, n:n + chunk], preferred_element_type=F32).astype(BF16)
    t_ref[0] = jnp.dot(hb, wt_ref[...], preferred_element_type=F32)


def _in_proj(x, mod, g_mix, w_big, w_tail):
    B, S, D = x.shape
    tm = 512
    return pl.pallas_call(
        functools.partial(_inproj_kernel, chunk=256),
        grid=(B, S // tm),
        in_specs=[pl.BlockSpec((1, tm, D), lambda b, i: (b, i, 0)),
                  pl.BlockSpec((1, N_MOD, D), lambda b, i: (b, 0, 0)),
                  pl.BlockSpec((1, D), lambda b, i: (0, 0)),
                  pl.BlockSpec((D, PROJ_W), lambda b, i: (0, 0)),
                  pl.BlockSpec((D, 128), lambda b, i: (0, 0))],
        out_specs=[pl.BlockSpec((1, tm, PROJ_W), lambda b, i: (b, i, 0)),
                   pl.BlockSpec((1, tm, 128), lambda b, i: (b, i, 0))],
        out_shape=[jax.ShapeDtypeStruct((B, S, PROJ_W), BF16),
                   jax.ShapeDtypeStruct((B, S, 128), F32)],
        compiler_params=_cparams(("arbitrary", "arbitrary")),
        name="in_proj",
    )(x, mod, g_mix.reshape(1, D), w_big, w_tail)


def _key_to_f32(k):
    bits = k ^ ((k >> 31) & 0x7FFFFFFF)
    return lax.bitcast_convert_type(bits, F32)


def _dsa_kernel(qa_ref, qi_ref, w_ref, ka_ref, vat_ref, kia_ref, kib_ref, o_ref,
                score_ref, wt_ref, lm_ref, p_ref, mb_ref, dist_ref, dist0_ref, m_ref, l_ref, acc_ref,
                *, TQ, TK, S, STEPS):
    i = pl.program_id(1)
    q0 = i * TQ
    n_ch = i + 1
    kf = float(TOPK_MAX)
    dn_nt = (((1,), (1,)), ((), ()))

    key_i = lax.broadcasted_iota(jnp.int32, (TK, TQ), 0)
    qry_i = lax.broadcasted_iota(jnp.int32, (TK, TQ), 1)
    t_idx = q0 + lax.broadcasted_iota(jnp.int32, (1, TQ), 1)

    dist0_ref[...] = (qry_i - key_i).astype(F32)
    wt_ref[...] = w_ref[0].T * IDX_W_SCALE

    def idx_chunk(kc, carry):
        k0 = pl.multiple_of(kc * TK, TK)
        kia = kia_ref[0, pl.ds(k0, TK), :]
        kib = kib_ref[0, pl.ds(k0, TK), :]
        acc = jnp.zeros((TK, TQ), F32)
        for j in range(IDX_HEADS // 2):
            qp = qi_ref[0, :, j * 128:(j + 1) * 128]
            sa = lax.dot_general(kia, qp, dn_nt, preferred_element_type=F32)
            sb = lax.dot_general(kib, qp, dn_nt, preferred_element_type=F32)
            acc = (acc + jnp.maximum(sa, 0.0) * wt_ref[2 * j:2 * j + 1, :]
                   + jnp.maximum(sb, 0.0) * wt_ref[2 * j + 1:2 * j + 2, :])
        causal = (k0 + key_i) <= (q0 + qry_i)
        score_ref[kc] = jnp.where(causal, acc, -jnp.inf)
        return carry

    lax.fori_loop(0, n_ch, idx_chunk, 0)

    def count_keys(pred):
        def body(kc, acc):
            m = pred(score_ref[kc], kc)
            return acc + m.reshape(TK // 32, 4, 8, TQ).sum(axis=0)

        acc = lax.fori_loop(0, n_ch, body, jnp.zeros((4, 8, TQ), F32))
        return jnp.sum(acc.sum(axis=0), axis=0, keepdims=True)

    def count_ge(t):
        return count_keys(lambda blk, kc: jnp.where(blk >= t, 1.0, 0.0))

    n_masked = (S - 1 - t_idx).astype(F32)
    n_causal = (t_idx + 1).astype(F32)

    def bis_cond(st):
        return jnp.logical_and(st[4] > 0.5, st[5] < 40)

    def bis_body(st):
        lo, hi, cnt_lo, cph_lo, _, it = st
        for _ in range(STEPS):
            mid = (lo >> 1) + (hi >> 1) + (lo & hi & 1)
            fm = _key_to_f32(mid)
            cph = count_ge(fm)
            c = cph + jnp.where(fm <= NEG, n_masked, 0.0)
            ge = c >= kf
            lo = jnp.where(ge, mid, lo)
            hi = jnp.where(ge, hi, mid)
            cnt_lo = jnp.where(ge, c, cnt_lo)
            cph_lo = jnp.where(ge, cph, cph_lo)
        done = jnp.logical_or(cnt_lo == kf, lo + 1 >= hi)
        return lo, hi, cnt_lo, cph_lo, jnp.max(jnp.where(done, 0.0, 1.0)), it + STEPS

    st0 = (jnp.full((1, TQ), KEY_LO0, jnp.int32), jnp.full((1, TQ), KEY_HI0, jnp.int32),
           jnp.full((1, TQ), float(S), F32), n_causal, jnp.float32(1.0), jnp.int32(0))
    lo, hi, cnt_lo, cph_lo, _, _ = lax.while_loop(bis_cond, bis_body, st0)
    thr = _key_to_f32(lo)

    f_hi = _key_to_f32(hi)
    masked_hi = jnp.where(f_hi <= NEG, n_masked, 0.0)
    need = jnp.logical_and(cnt_lo > kf, cph_lo > kf - masked_hi)
    any_need = jnp.max(jnp.where(need, 1.0, 0.0)) > 0.5

    @pl.when(any_need)
    def _fix_ties():
        slots = kf - count_ge(f_hi) - masked_hi

        def jb(_, st):
            jlo, jhi = st
            jm = (jlo + jhi) >> 1
            ties = count_keys(lambda blk, kc: jnp.where(
                blk == thr, jnp.where(kc * TK + key_i <= jm, 1.0, 0.0), 0.0))
            ge = ties >= slots
            return jnp.where(ge, jlo, jm), jnp.where(ge, jm, jhi)

        n_iter = max(1, (S - 1).bit_length())
        _, jcut = lax.fori_loop(0, n_iter, jb, (jnp.full((1, TQ), -1, jnp.int32),
                                                 jnp.full((1, TQ), S - 1, jnp.int32)))
        jcut = jnp.where(need, jcut, S)

        def drop(kc, carry):
            blk = score_ref[kc]
            surplus = jnp.where(blk == thr, jnp.where(kc * TK + key_i > jcut, 1.0, 0.0), 0.0)
            score_ref[kc] = jnp.where(surplus > 0.5, -jnp.inf, blk)
            return carry

        lax.fori_loop(0, n_ch, drop, 0)

    m_ref[...] = jnp.full(m_ref.shape, NEG, F32)
    l_ref[...] = jnp.zeros(l_ref.shape, F32)
    acc_ref[...] = jnp.zeros(acc_ref.shape, F32)

    def att_chunk(kc, carry):
        k0 = pl.multiple_of(kc * TK, TK)
        mb_ref[...] = jnp.where(score_ref[kc] >= thr, 0.0, NEG)
        dist_ref[...] = dist0_ref[...] + (q0 - k0).astype(F32)
        for hh in range(N_HEADS_A):
            g = hh // HEADS_PER_KV
            kg = ka_ref[0, pl.ds(k0, TK), g * HEAD_DIM:(g + 1) * HEAD_DIM]
            qh = qa_ref[0, :, hh * HEAD_DIM:(hh + 1) * HEAD_DIM]
            lm_ref[hh] = lax.dot_general(kg, qh, dn_nt, preferred_element_type=F32)
        for hh in range(N_HEADS_A):
            g = hh // HEADS_PER_KV
            vt = vat_ref[0, kc, g * HEAD_DIM:(g + 1) * HEAD_DIM, :]
            lm = lm_ref[hh] * (ATTN_SCALE * LOG2E) - (SLOPES_A[hh] * LOG2E) * dist_ref[...] + mb_ref[...]
            lm_ref[hh] = lm
            m_old = m_ref[hh]
            m_new = jnp.maximum(m_old, jnp.max(lm, axis=0, keepdims=True))
            alpha = jnp.exp2(m_old - m_new)
            m_ref[hh] = m_new
            p = jnp.exp2(lm_ref[hh] - m_new)
            l_ref[hh] = alpha * l_ref[hh] + jnp.sum(p, axis=0, keepdims=True)
            p_ref[hh] = p.astype(BF16)
            pv = jnp.dot(vt, p_ref[hh], preferred_element_type=F32)
            acc_ref[hh] = alpha * acc_ref[hh] + pv
        return carry

    lax.fori_loop(0, n_ch, att_chunk, 0)

    for hh in range(N_HEADS_A):
        o_t = acc_ref[hh] / l_ref[hh]
        o_ref[0, :, hh * HEAD_DIM:(hh + 1) * HEAD_DIM] = o_t.T.astype(BF16)


def _dsa_attention(proj, tail):
    B, S, _ = proj.shape
    TQ = TK = 256
    n_kc = S // TK
    vat = jnp.swapaxes(proj[:, :, COL_VA:COL_VA + N_KV * HEAD_DIM].reshape(B, n_kc, TK, N_KV * HEAD_DIM), 2, 3)
    kern = functools.partial(_dsa_kernel, TQ=TQ, TK=TK, S=S, STEPS=2)
    return pl.pallas_call(
        kern,
        grid=(B, S // TQ),
        in_specs=[pl.BlockSpec((1, TQ, 1024), lambda b, i: (b, i, COL_QA // 1024)),
                  pl.BlockSpec((1, TQ, 1024), lambda b, i: (b, i, COL_QI // 1024)),
                  pl.BlockSpec((1, TQ, 128), lambda b, i: (b, i, 0)),
                  pl.BlockSpec((1, S, 256), lambda b, i: (b, 0, COL_KA // 256)),
                  pl.BlockSpec((1, n_kc, N_KV * HEAD_DIM, TK), lambda b, i: (b, 0, 0, 0)),
                  pl.BlockSpec((1, S, 128), lambda b, i: (b, 0, COL_KIA // 128)),
                  pl.BlockSpec((1, S, 128), lambda b, i: (b, 0, COL_KIB // 128))],
        out_specs=pl.BlockSpec((1, TQ, 1024), lambda b, i: (b, i, 0)),
        out_shape=jax.ShapeDtypeStruct((B, S, N_HEADS_A * HEAD_DIM), BF16),
        scratch_shapes=[pltpu.VMEM((n_kc, TK, TQ), F32),
                        pltpu.VMEM((128, TQ), F32),
                        pltpu.VMEM((N_HEADS_A, TK, TQ), F32),
                        pltpu.VMEM((N_HEADS_A, TK, TQ), BF16),
                        pltpu.VMEM((TK, TQ), F32),
                        pltpu.VMEM((TK, TQ), F32),
                        pltpu.VMEM((TK, TQ), F32),
                        pltpu.VMEM((N_HEADS_A, 1, TQ), F32),
                        pltpu.VMEM((N_HEADS_A, 1, TQ), F32),
                        pltpu.VMEM((N_HEADS_A, HEAD_DIM, TQ), F32)],
        compiler_params=_cparams(("arbitrary", "arbitrary")),
        name="dsa_attn",
    )(proj, proj, tail, proj, vat, proj, proj)


def _swa_kernel(sink_ref, q_ref, k_ref, v_ref, o_ref, qst_ref, p_ref, *, TQ, KW):
    i = pl.program_id(1)
    q0 = i * TQ
    start = pl.multiple_of(jnp.maximum(q0 - WINDOW, 0), WINDOW)
    dn_nt = (((1,), (1,)), ((), ()))
    row_i = lax.broadcasted_iota(jnp.int32, (TQ, KW), 0)
    col_i = lax.broadcasted_iota(jnp.int32, (TQ, KW), 1)
    dist_i = (q0 + row_i) - (start + col_i)
    valid = jnp.logical_and(dist_i >= 0, dist_i < WINDOW)
    dist = dist_i.astype(F32)
    for g in range(N_KV):
        for r in range(HEADS_PER_KV):
            hh = g * HEADS_PER_KV + r
            qst_ref[r * TQ:(r + 1) * TQ, :] = q_ref[0, :, hh * HEAD_DIM:(hh + 1) * HEAD_DIM]
        kg = k_ref[0, pl.ds(start, KW), g * HEAD_DIM:(g + 1) * HEAD_DIM]
        vg = v_ref[0, pl.ds(start, KW), g * HEAD_DIM:(g + 1) * HEAD_DIM]
        s_all = lax.dot_general(qst_ref[...], kg, dn_nt, preferred_element_type=F32)
        for r in range(HEADS_PER_KV):
            hh = g * HEADS_PER_KV + r
            sink = sink_ref[hh]
            lg = s_all[r * TQ:(r + 1) * TQ] * ATTN_SCALE - SLOPES_B[hh] * dist
            lm = jnp.where(valid, lg, NEG)
            m = jnp.maximum(jnp.max(lm, axis=1, keepdims=True), sink)
            e = jnp.exp(lm - m)
            den = jnp.sum(e, axis=1, keepdims=True) + jnp.exp(sink - m)
            p_ref[r * TQ:(r + 1) * TQ, :] = (e / den).astype(BF16)
        pv = jnp.dot(p_ref[...], vg, preferred_element_type=F32)
        for r in range(HEADS_PER_KV):
            hh = g * HEADS_PER_KV + r
            o_ref[0, :, hh * HEAD_DIM:(hh + 1) * HEAD_DIM] = pv[r * TQ:(r + 1) * TQ].astype(BF16)


def _swa_attention(proj, sinks):
    B, S, _ = proj.shape
    TQ = 256
    KW = TQ + WINDOW
    kern = functools.partial(_swa_kernel, TQ=TQ, KW=KW)
    return pl.pallas_call(
        kern,
        grid=(B, S // TQ),
        in_specs=[pl.BlockSpec(memory_space=pltpu.SMEM),
                  pl.BlockSpec((1, TQ, 1024), lambda b, i: (b, i, COL_QB // 1024)),
                  pl.BlockSpec((1, S, 256), lambda b, i: (b, 0, COL_KB // 256)),
                  pl.BlockSpec((1, S, 256), lambda b, i: (b, 0, COL_VB // 256))],
        out_specs=pl.BlockSpec((1, TQ, 1024), lambda b, i: (b, i, 0)),
        out_shape=jax.ShapeDtypeStruct((B, S, N_HEADS_B * HEAD_DIM), BF16),
        scratch_shapes=[pltpu.VMEM((HEADS_PER_KV * TQ, HEAD_DIM), BF16),
                        pltpu.VMEM((HEADS_PER_KV * TQ, KW), BF16)],
        compiler_params=_cparams(("arbitrary", "arbitrary")),
        name="swa_attn",
    )(sinks, proj, proj, proj)


def _outproj_kernel(oa_ref, ob_ref, x_ref, mod_ref, g_ref, w_ref, x1_ref, h2_ref):
    half = oa_ref.shape[2]
    mix = jnp.dot(oa_ref[0], w_ref[:half, :], preferred_element_type=F32)
    mix = mix + jnp.dot(ob_ref[0], w_ref[half:, :], preferred_element_type=F32)
    x1 = x_ref[0] + mod_ref[0, 2:3, :] * mix
    x1_ref[0] = x1
    ms = jnp.mean(x1 * x1, axis=-1, keepdims=True)
    y = x1 * lax.rsqrt(ms + EPS) * g_ref[...]
    h2_ref[0] = (y * (1.0 + mod_ref[0, 4:5, :]) + mod_ref[0, 3:4, :]).astype(BF16)


def _out_proj(oa, ob, x, mod, g_ffn, w_o):
    B, S, D = x.shape
    tm = 512
    half = oa.shape[2]
    return pl.pallas_call(
        _outproj_kernel,
        grid=(B, S // tm),
        in_specs=[pl.BlockSpec((1, tm, half), lambda b, i: (b, i, 0)),
                  pl.BlockSpec((1, tm, half), lambda b, i: (b, i, 0)),
                  pl.BlockSpec((1, tm, D), lambda b, i: (b, i, 0)),
                  pl.BlockSpec((1, N_MOD, D), lambda b, i: (b, 0, 0)),
                  pl.BlockSpec((1, D), lambda b, i: (0, 0)),
                  pl.BlockSpec((2 * half, D), lambda b, i: (0, 0))],
        out_specs=[pl.BlockSpec((1, tm, D), lambda b, i: (b, i, 0)),
                   pl.BlockSpec((1, tm, D), lambda b, i: (b, i, 0))],
        out_shape=[jax.ShapeDtypeStruct((B, S, D), F32),
                   jax.ShapeDtypeStruct((B, S, D), BF16)],
        compiler_params=_cparams(("arbitrary", "arbitrary")),
        name="out_proj",
    )(oa, ob, x, mod, g_ffn.reshape(1, D), w_o)


def _ffn_kernel(h2_ref, x1_ref, mod_ref, wg_ref, wu_ref, cw_ref, cb_ref, wd_ref, gf_ref, o_ref,
                acc_ref, carry_ref, *, tm):
    i = pl.program_id(1)
    j = pl.program_id(2)
    nj = pl.num_programs(2)

    @pl.when(j == 0)
    def _():
        acc_ref[...] = jnp.zeros(acc_ref.shape, F32)

    h2 = h2_ref[0]
    g = jnp.dot(h2, wg_ref[...], preferred_element_type=F32)
    u = jnp.dot(h2, wu_ref[...], preferred_element_type=F32)

    prev = jnp.where(i > 0, carry_ref[j], 0.0)
    carry_ref[j] = g[tm - 8:, :]
    row = lax.broadcasted_iota(jnp.int32, g.shape, 0)
    g1 = jnp.where(row == 0, prev[7:8, :], pltpu.roll(g, 1, axis=0))
    g2 = jnp.where(row == 0, prev[6:7, :], jnp.where(row == 1, prev[7:8, :], pltpu.roll(g, 2, axis=0)))
    gc = cw_ref[0:1, :] * g2 + cw_ref[1:2, :] * g1 + cw_ref[2:3, :] * g + cb_ref[...]
    a = gc / (1.0 + jnp.exp(-gc)) * u
    acc_ref[...] += jnp.dot(a.astype(BF16), wd_ref[...], preferred_element_type=F32)

    @pl.when(j == nj - 1)
    def _():
        x2 = x1_ref[0] + mod_ref[0, 5:6, :] * acc_ref[...]
        ms = jnp.mean(x2 * x2, axis=-1, keepdims=True)
        o_ref[0] = x2 * lax.rsqrt(ms + EPS) * gf_ref[...]


def _conv_ffn(h2, x1, mod, w_gate, w_up, conv_w, conv_b, w_down, g_final):
    B, S, D = x1.shape
    F = w_gate.shape[1]
    tm, tf = 512, 512
    kern = functools.partial(_ffn_kernel, tm=tm)
    return pl.pallas_call(
        kern,
        grid=(B, S // tm, F // tf),
        in_specs=[pl.BlockSpec((1, tm, D), lambda b, i, j: (b, i, 0)),
                  pl.BlockSpec((1, tm, D), lambda b, i, j: (b, i, 0)),
                  pl.BlockSpec((1, N_MOD, D), lambda b, i, j: (b, 0, 0)),
                  pl.BlockSpec((D, tf), lambda b, i, j: (0, j)),
                  pl.BlockSpec((D, tf), lambda b, i, j: (0, j)),
                  pl.BlockSpec((3, tf), lambda b, i, j: (0, j)),
                  pl.BlockSpec((1, tf), lambda b, i, j: (0, j)),
                  pl.BlockSpec((tf, D), lambda b, i, j: (j, 0)),
                  pl.BlockSpec((1, D), lambda b, i, j: (0, 0))],
        out_specs=pl.BlockSpec((1, tm, D), lambda b, i, j: (b, i, 0)),
        out_shape=jax.ShapeDtypeStruct((B, S, D), F32),
        scratch_shapes=[pltpu.VMEM((tm, D), F32),
                        pltpu.VMEM((F // tf, 8, tf), F32)],
        compiler_params=_cparams(("arbitrary", "arbitrary", "arbitrary")),
        name="conv_ffn",
    )(h2, x1, mod, w_gate, w_up, conv_w, conv_b.reshape(1, F), w_down, g_final.reshape(1, D))


def _regroup_w_in(w_in):
    sizes = [1024, 256, 256, 1024, 64, 16, 1024, 256, 256]
    offs = [0]
    for s in sizes:
        offs.append(offs[-1] + s)
    qa, ka, va, qi, ki, wi, qb, kb, vb = [w_in[:, offs[n]:offs[n + 1]] for n in range(9)]
    z64 = jnp.zeros_like(ki)
    big = jnp.concatenate([qa, qi, qb, ka, va, kb, vb, ki, z64, z64, ki], axis=1).astype(BF16)
    tail = jnp.concatenate([wi, jnp.zeros((w_in.shape[0], 128 - IDX_HEADS), w_in.dtype)], axis=1).astype(BF16)
    return big, tail


def kernel(x, c, w_ada, b_ada, g_mix, w_in, sinks, w_o, g_ffn, w_gate, w_up, conv_w, conv_b, w_down, g_final):
    assert w_ada.shape[0] == 1, "the final norm is fused into the (single) layer's FFN kernel"
    mod = _ada_mod(c, w_ada[0], b_ada[0])
    w_big, w_tail = _regroup_w_in(w_in[0])
    proj, tail = _in_proj(x, mod, g_mix[0], w_big, w_tail)
    oa = _dsa_attention(proj, tail)
    ob = _swa_attention(proj, sinks[0])
    x1, h2 = _out_proj(oa, ob, x, mod, g_ffn[0], w_o[0].astype(BF16))
    return _conv_ffn(h2, x1, mod, w_gate[0].astype(BF16), w_up[0].astype(BF16),
                     conv_w[0], conv_b[0], w_down[0].astype(BF16), g_final)
```

```python
import functools

import jax
import jax.numpy as jnp
from jax import lax
from jax.experimental import pallas as pl
from jax.experimental.pallas import tpu as pltpu

F32 = jnp.float32
BF16 = jnp.bfloat16

D_MODEL = 2048
HEAD_DIM = 128
N_HEADS_A = 8
N_HEADS_B = 8
N_KV = 2
HEADS_PER_KV = 4
IDX_HEADS = 16
IDX_DIM = 64
TOPK_MAX = 256
WINDOW = 128
D_FF = 5632
N_MOD = 6
EPS = 1e-6
NEG = -1e30
ATTN_SCALE = HEAD_DIM ** -0.5
LOG2E = 1.4426950408889634
IDX_W_SCALE = (IDX_DIM ** -0.5) * (IDX_HEADS ** -0.5)
SLOPES = [2.0 ** (-8.0 * j / 16.0) for j in range(1, 17)]
SLOPES_A = SLOPES[0::2]
SLOPES_B = SLOPES[1::2]

COL_QA, COL_QI, COL_QB = 0, 1024, 2048
COL_KA, COL_VA, COL_KB, COL_VB = 3072, 3328, 3584, 3840
COL_KIA, COL_KIB = 4096, 4224
PROJ_W = 4352

VMEM_LIMIT = 56 * 1024 * 1024

KEY_LO0 = -2139095040
KEY_HI0 = 2139095040
F32_MIN_NORMAL = 1.1754943508222875e-38
F32_MIN_NORMAL_KEY = 0x00800000


def _cparams(sem):
    return pltpu.CompilerParams(dimension_semantics=sem, vmem_limit_bytes=VMEM_LIMIT)


def _ada_kernel(c_ref, w_ref, b_ref, o_ref):
    c = c_ref[...]
    s = c / (1.0 + jnp.exp(-c))
    o_ref[...] = jnp.dot(s.astype(BF16), w_ref[...].astype(BF16),
                         preferred_element_type=F32) + b_ref[...]


def _ada_mod(c, w_ada, b_ada):
    B, D = c.shape
    N = w_ada.shape[1]
    tn = 1024
    cp = jnp.zeros((8, D), F32).at[:B].set(c)
    out = pl.pallas_call(
        _ada_kernel,
        grid=(N // tn,),
        in_specs=[pl.BlockSpec((8, D), lambda j: (0, 0)),
                  pl.BlockSpec((D, tn), lambda j: (0, j)),
                  pl.BlockSpec((1, tn), lambda j: (0, j))],
        out_specs=pl.BlockSpec((8, tn), lambda j: (0, j)),
        out_shape=jax.ShapeDtypeStruct((8, N), F32),
        compiler_params=_cparams(("arbitrary",)),
        name="ada_mod",
    )(cp, w_ada, b_ada.reshape(1, N))
    return out[:B].reshape(B, N_MOD, D)


def _inproj_kernel(x_ref, mod_ref, g_ref, w_ref, wt_ref, p_ref, t_ref, *, chunk):
    x = x_ref[0]
    ms = jnp.mean(x * x, axis=-1, keepdims=True)
    y = x * lax.rsqrt(ms + EPS) * g_ref[...]
    h = y * (1.0 + mod_ref[0, 1:2, :]) + mod_ref[0, 0:1, :]
    hb = h.astype(BF16)
    for n in range(0, PROJ_W, chunk):
        p_ref[0, :, n:n + chunk] = jnp.dot(
            hb, w_ref[:, n:n + chunk], preferred_element_type=F32).astype(BF16)
    t_ref[0] = jnp.dot(hb, wt_ref[...], preferred_element_type=F32)


def _in_proj(x, mod, g_mix, w_big, w_tail):
    B, S, D = x.shape
    tm = 512
    return pl.pallas_call(
        functools.partial(_inproj_kernel, chunk=256),
        grid=(B, S // tm),
        in_specs=[pl.BlockSpec((1, tm, D), lambda b, i: (b, i, 0)),
                  pl.BlockSpec((1, N_MOD, D), lambda b, i: (b, 0, 0)),
                  pl.BlockSpec((1, D), lambda b, i: (0, 0)),
                  pl.BlockSpec((D, PROJ_W), lambda b, i: (0, 0)),
                  pl.BlockSpec((D, 128), lambda b, i: (0, 0))],
        out_specs=[pl.BlockSpec((1, tm, PROJ_W), lambda b, i: (b, i, 0)),
                   pl.BlockSpec((1, tm, 128), lambda b, i: (b, i, 0))],
        out_shape=[jax.ShapeDtypeStruct((B, S, PROJ_W), BF16),
                   jax.ShapeDtypeStruct((B, S, 128), F32)],
        compiler_params=_cparams(("arbitrary", "arbitrary")),
        name="in_proj",
    )(x, mod, g_mix.reshape(1, D), w_big, w_tail)


def _key_to_f32(k):
    bits = k ^ ((k >> 31) & 0x7FFFFFFF)
    return lax.bitcast_convert_type(bits, F32)


def _f32_to_key(f):
    bits = lax.bitcast_convert_type(f, jnp.int32)
    return bits ^ ((bits >> 31) & 0x7FFFFFFF)


def _dsa_kernel(qa_ref, qi_ref, w_ref, ka_ref, vat_ref, kia_ref, kib_ref, kpos_ref, qslope_ref, o_ref,
                score_ref, wt_ref, lm_ref, p_ref, mb_ref, m_ref, l_ref, acc_ref,
                *, TQ, TK, S, STEPS):
    i = pl.program_id(1)
    q0 = i * TQ
    n_ch = i + 1
    kf = float(TOPK_MAX)
    dn_nt = (((1,), (1,)), ((), ()))

    key_i = lax.broadcasted_iota(jnp.int32, (TK, TQ), 0)
    qry_i = lax.broadcasted_iota(jnp.int32, (TK, TQ), 1)
    t_idx = q0 + lax.broadcasted_iota(jnp.int32, (1, TQ), 1)

    wt_ref[...] = w_ref[0].T * IDX_W_SCALE

    def idx_chunk(kc, carry):
        k0 = pl.multiple_of(kc * TK, TK)
        kia = kia_ref[0, pl.ds(k0, TK), :]
        kib = kib_ref[0, pl.ds(k0, TK), :]
        acc = jnp.zeros((TK, TQ), F32)
        for j in range(IDX_HEADS // 2):
            qp = qi_ref[0, :, j * 128:(j + 1) * 128]
            sa = lax.dot_general(kia, qp, dn_nt, preferred_element_type=F32)
            sb = lax.dot_general(kib, qp, dn_nt, preferred_element_type=F32)
            acc = (acc + jnp.maximum(sa, 0.0) * wt_ref[2 * j:2 * j + 1, :]
                   + jnp.maximum(sb, 0.0) * wt_ref[2 * j + 1:2 * j + 2, :])
        causal = (k0 + key_i) <= (q0 + qry_i)
        score_ref[kc] = jnp.where(causal, acc, -jnp.inf)
        smin, smax = carry
        smin = jnp.minimum(smin, jnp.min(jnp.where(causal, acc, jnp.inf), axis=0, keepdims=True))
        smax = jnp.maximum(smax, jnp.max(jnp.where(causal, acc, -jnp.inf), axis=0, keepdims=True))
        return smin, smax

    smin, smax = lax.fori_loop(0, n_ch, idx_chunk, (jnp.full((1, TQ), jnp.inf, F32),
                                                    jnp.full((1, TQ), -jnp.inf, F32)))

    def count_keys(pred):
        def body(kc, acc):
            m = pred(score_ref[kc], kc)
            return acc + m.reshape(TK // 32, 4, 8, TQ).sum(axis=0)

        acc = lax.fori_loop(0, n_ch, body, jnp.zeros((4, 8, TQ), F32))
        return jnp.sum(acc.sum(axis=0), axis=0, keepdims=True)

    def count_ge(t):
        return count_keys(lambda blk, kc: jnp.where(blk >= t, 1.0, 0.0))

    n_masked = (S - 1 - t_idx).astype(F32)
    n_causal = (t_idx + 1).astype(F32)

    def bis_cond(st):
        return jnp.logical_and(st[4] > 0.5, st[5] < 2 * 32 + 8)

    def bis_body(st):
        lo, hi, cnt_lo, cph_lo, _, it = st
        for step in range(STEPS):
            mid = (lo >> 1) + (hi >> 1) + (lo & hi & 1)
            if step % 2 == 0:
                f_lo, f_hi = _key_to_f32(lo), _key_to_f32(hi)
                mid_v = _f32_to_key(f_lo + (f_hi - f_lo) * 0.5)
                mid = jnp.where(jnp.logical_and(mid_v > lo, mid_v < hi), mid_v, mid)
            fm = _key_to_f32(mid)
            cph = count_ge(fm)
            c = cph + jnp.where(fm <= NEG, n_masked, 0.0)
            ge = c >= kf
            lo = jnp.where(ge, mid, lo)
            hi = jnp.where(ge, hi, mid)
            cnt_lo = jnp.where(ge, c, cnt_lo)
            cph_lo = jnp.where(ge, cph, cph_lo)
        done = jnp.logical_or(cnt_lo == kf, lo + 1 >= hi)
        return lo, hi, cnt_lo, cph_lo, jnp.max(jnp.where(done, 0.0, 1.0)), it + STEPS

    below_neg = smin < NEG
    few = jnp.logical_and(n_causal < kf, jnp.logical_not(below_neg))
    key_neg = _f32_to_key(jnp.full((1, TQ), NEG, F32))
    lo0 = jnp.where(below_neg, KEY_LO0, jnp.where(few, key_neg, _f32_to_key(smin)))
    above_max = jnp.where(jnp.abs(smax) < F32_MIN_NORMAL, F32_MIN_NORMAL_KEY, _f32_to_key(smax) + 1)
    hi0 = jnp.where(below_neg, KEY_HI0, jnp.where(few, key_neg + 1, above_max))
    cnt0 = n_causal + jnp.where(_key_to_f32(lo0) <= NEG, n_masked, 0.0)
    st0 = (lo0, hi0, cnt0, n_causal, jnp.float32(1.0), jnp.int32(0))
    lo, hi, cnt_lo, cph_lo, _, _ = lax.while_loop(bis_cond, bis_body, st0)
    thr = _key_to_f32(lo)

    f_hi = _key_to_f32(hi)
    masked_hi = jnp.where(f_hi <= NEG, n_masked, 0.0)
    need = jnp.logical_and(cnt_lo > kf, cph_lo > kf - masked_hi)
    any_need = jnp.max(jnp.where(need, 1.0, 0.0)) > 0.5

    @pl.when(any_need)
    def _fix_ties():
        slots = kf - count_ge(f_hi) - masked_hi

        def jb(_, st):
            jlo, jhi = st
            jm = (jlo + jhi) >> 1
            ties = count_keys(lambda blk, kc: jnp.where(
                blk == thr, jnp.where(kc * TK + key_i <= jm, 1.0, 0.0), 0.0))
            ge = ties >= slots
            return jnp.where(ge, jlo, jm), jnp.where(ge, jm, jhi)

        n_iter = max(1, (S - 1).bit_length())
        _, jcut = lax.fori_loop(0, n_iter, jb, (jnp.full((1, TQ), -1, jnp.int32),
                                                 jnp.full((1, TQ), S - 1, jnp.int32)))
        jcut = jnp.where(need, jcut, S)

        def drop(kc, carry):
            blk = score_ref[kc]
            surplus = jnp.where(blk == thr, jnp.where(kc * TK + key_i > jcut, 1.0, 0.0), 0.0)
            score_ref[kc] = jnp.where(surplus > 0.5, -jnp.inf, blk)
            return carry

        lax.fori_loop(0, n_ch, drop, 0)

    m_ref[...] = jnp.full(m_ref.shape, NEG, F32)
    l_ref[...] = jnp.zeros(l_ref.shape, F32)
    acc_ref[...] = jnp.zeros(acc_ref.shape, F32)

    def att_chunk(kc, carry):
        k0 = pl.multiple_of(kc * TK, TK)
        mb_ref[...] = jnp.where(score_ref[kc] >= thr, 0.0, NEG)
        kpos = kpos_ref[pl.ds(k0, TK), :]
        for hh in range(N_HEADS_A):
            g = hh // HEADS_PER_KV
            kg = ka_ref[0, pl.ds(k0, TK), g * HEAD_DIM:(g + 1) * HEAD_DIM]
            qh = qa_ref[0, :, hh * HEAD_DIM:(hh + 1) * HEAD_DIM]
            lm_ref[hh] = lax.dot_general(jnp.concatenate([kg, kpos], axis=1),
                                         jnp.concatenate([qh, qslope_ref[hh]], axis=1),
                                         dn_nt, preferred_element_type=F32)
        for hh in range(N_HEADS_A):
            g = hh // HEADS_PER_KV
            vt = vat_ref[0, kc, g * HEAD_DIM:(g + 1) * HEAD_DIM, :]
            lm = lm_ref[hh] * (ATTN_SCALE * LOG2E) + mb_ref[...]
            lm_ref[hh] = lm
            m_old = m_ref[hh]
            m_new = jnp.maximum(m_old, jnp.max(lm, axis=0, keepdims=True))
            alpha = jnp.exp2(m_old - m_new)
            m_ref[hh] = m_new
            p = jnp.exp2(lm_ref[hh] - m_new)
            l_ref[hh] = alpha * l_ref[hh] + jnp.sum(p, axis=0, keepdims=True)
            p_ref[hh] = p.astype(BF16)
            pv = jnp.dot(vt, p_ref[hh], preferred_element_type=F32)
            acc_ref[hh] = alpha * acc_ref[hh] + pv
        return carry

    lax.fori_loop(0, n_ch, att_chunk, 0)

    for hh in range(N_HEADS_A):
        o_t = acc_ref[hh] / l_ref[hh]
        o_ref[0, :, hh * HEAD_DIM:(hh + 1) * HEAD_DIM] = o_t.T.astype(BF16)


POS_RADIX = 64
N_SPLIT = 3


def _alibi_operands(S, TQ):
    assert S <= POS_RADIX * POS_RADIX
    s = jnp.arange(S, dtype=jnp.int32)
    digits = [(s // POS_RADIX).astype(F32)] * N_SPLIT + [(s % POS_RADIX).astype(F32)] * N_SPLIT
    kpos = jnp.zeros((S, 128), F32).at[:, :2 * N_SPLIT].set(jnp.stack(digits, axis=1)).astype(BF16)
    rows = []
    for slope in SLOPES_A:
        rest, pieces = jnp.float32(slope / ATTN_SCALE), []
        for _ in range(N_SPLIT):
            piece = rest.astype(BF16).astype(F32)
            pieces.append(piece)
            rest = rest - piece
        rows.append(jnp.stack([p * POS_RADIX for p in pieces] + pieces))
    qslope = jnp.zeros((N_HEADS_A, 128), F32).at[:, :2 * N_SPLIT].set(jnp.stack(rows)).astype(BF16)
    return kpos, jnp.broadcast_to(qslope[:, None, :], (N_HEADS_A, TQ, 128))


def _dsa_attention(proj, tail):
    B, S, _ = proj.shape
    TQ = TK = 256
    n_kc = S // TK
    vat = jnp.swapaxes(proj[:, :, COL_VA:COL_VA + N_KV * HEAD_DIM].reshape(B, n_kc, TK, N_KV * HEAD_DIM), 2, 3)
    kpos, qslope = _alibi_operands(S, TQ)
    kern = functools.partial(_dsa_kernel, TQ=TQ, TK=TK, S=S, STEPS=2)
    return pl.pallas_call(
        kern,
        grid=(B, S // TQ),
        in_specs=[pl.BlockSpec((1, TQ, 1024), lambda b, i: (b, i, COL_QA // 1024)),
                  pl.BlockSpec((1, TQ, 1024), lambda b, i: (b, i, COL_QI // 1024)),
                  pl.BlockSpec((1, TQ, 128), lambda b, i: (b, i, 0)),
                  pl.BlockSpec((1, S, 256), lambda b, i: (b, 0, COL_KA // 256)),
                  pl.BlockSpec((1, n_kc, N_KV * HEAD_DIM, TK), lambda b, i: (b, 0, 0, 0)),
                  pl.BlockSpec((1, S, 128), lambda b, i: (b, 0, COL_KIA // 128)),
                  pl.BlockSpec((1, S, 128), lambda b, i: (b, 0, COL_KIB // 128)),
                  pl.BlockSpec((S, 128), lambda b, i: (0, 0)),
                  pl.BlockSpec((N_HEADS_A, TQ, 128), lambda b, i: (0, 0, 0))],
        out_specs=pl.BlockSpec((1, TQ, 1024), lambda b, i: (b, i, 0)),
        out_shape=jax.ShapeDtypeStruct((B, S, N_HEADS_A * HEAD_DIM), BF16),
        scratch_shapes=[pltpu.VMEM((n_kc, TK, TQ), F32),
                        pltpu.VMEM((128, TQ), F32),
                        pltpu.VMEM((N_HEADS_A, TK, TQ), F32),
                        pltpu.VMEM((N_HEADS_A, TK, TQ), BF16),
                        pltpu.VMEM((TK, TQ), F32),
                        pltpu.VMEM((N_HEADS_A, 1, TQ), F32),
                        pltpu.VMEM((N_HEADS_A, 1, TQ), F32),
                        pltpu.VMEM((N_HEADS_A, HEAD_DIM, TQ), F32)],
        compiler_params=_cparams(("arbitrary", "arbitrary")),
        name="dsa_attn",
    )(proj, proj, tail, proj, vat, proj, proj, kpos, qslope)


def _swa_kernel(sink_ref, q_ref, k_ref, v_ref, o_ref, qst_ref, p_ref, *, TQ, KW):
    i = pl.program_id(1)
    q0 = i * TQ
    start = pl.multiple_of(jnp.maximum(q0 - WINDOW, 0), WINDOW)
    dn_nt = (((1,), (1,)), ((), ()))
    row_i = lax.broadcasted_iota(jnp.int32, (TQ, KW), 0)
    col_i = lax.broadcasted_iota(jnp.int32, (TQ, KW), 1)
    dist_i = (q0 + row_i) - (start + col_i)
    valid = jnp.logical_and(dist_i >= 0, dist_i < WINDOW)
    dist = dist_i.astype(F32)
    for g in range(N_KV):
        for r in range(HEADS_PER_KV):
            hh = g * HEADS_PER_KV + r
            qst_ref[r * TQ:(r + 1) * TQ, :] = q_ref[0, :, hh * HEAD_DIM:(hh + 1) * HEAD_DIM]
        kg = k_ref[0, pl.ds(start, KW), g * HEAD_DIM:(g + 1) * HEAD_DIM]
        vg = v_ref[0, pl.ds(start, KW), g * HEAD_DIM:(g + 1) * HEAD_DIM]
        s_all = lax.dot_general(qst_ref[...], kg, dn_nt, preferred_element_type=F32)
        for r in range(HEADS_PER_KV):
            hh = g * HEADS_PER_KV + r
            sink = sink_ref[hh]
            lg = s_all[r * TQ:(r + 1) * TQ] * ATTN_SCALE - SLOPES_B[hh] * dist
            lm = jnp.where(valid, lg, NEG)
            m = jnp.maximum(jnp.max(lm, axis=1, keepdims=True), sink)
            e = jnp.exp(lm - m)
            den = jnp.sum(e, axis=1, keepdims=True) + jnp.exp(sink - m)
            p_ref[r * TQ:(r + 1) * TQ, :] = (e / den).astype(BF16)
        pv = jnp.dot(p_ref[...], vg, preferred_element_type=F32)
        for r in range(HEADS_PER_KV):
            hh = g * HEADS_PER_KV + r
            o_ref[0, :, hh * HEAD_DIM:(hh + 1) * HEAD_DIM] = pv[r * TQ:(r + 1) * TQ].astype(BF16)


def _swa_attention(proj, sinks):
    B, S, _ = proj.shape
    TQ = 256
    KW = TQ + WINDOW
    kern = functools.partial(_swa_kernel, TQ=TQ, KW=KW)
    return pl.pallas_call(
        kern,
        grid=(B, S // TQ),
        in_specs=[pl.BlockSpec(memory_space=pltpu.SMEM),
                  pl.BlockSpec((1, TQ, 1024), lambda b, i: (b, i, COL_QB // 1024)),
                  pl.BlockSpec((1, S, 256), lambda b, i: (b, 0, COL_KB // 256)),
                  pl.BlockSpec((1, S, 256), lambda b, i: (b, 0, COL_VB // 256))],
        out_specs=pl.BlockSpec((1, TQ, 1024), lambda b, i: (b, i, 0)),
        out_shape=jax.ShapeDtypeStruct((B, S, N_HEADS_B * HEAD_DIM), BF16),
        scratch_shapes=[pltpu.VMEM((HEADS_PER_KV * TQ, HEAD_DIM), BF16),
                        pltpu.VMEM((HEADS_PER_KV * TQ, KW), BF16)],
        compiler_params=_cparams(("arbitrary", "arbitrary")),
        name="swa_attn",
    )(sinks, proj, proj, proj)


def _outproj_kernel(oa_ref, ob_ref, x_ref, mod_ref, g_ref, w_ref, x1_ref, h2_ref):
    half = oa_ref.shape[2]
    mix = jnp.dot(oa_ref[0], w_ref[:half, :], preferred_element_type=F32)
    mix = mix + jnp.dot(ob_ref[0], w_ref[half:, :], preferred_element_type=F32)
    x1 = x_ref[0] + mod_ref[0, 2:3, :] * mix
    x1_ref[0] = x1
    ms = jnp.mean(x1 * x1, axis=-1, keepdims=True)
    y = x1 * lax.rsqrt(ms + EPS) * g_ref[...]
    h2_ref[0] = (y * (1.0 + mod_ref[0, 4:5, :]) + mod_ref[0, 3:4, :]).astype(BF16)


def _out_proj(oa, ob, x, mod, g_ffn, w_o):
    B, S, D = x.shape
    tm = 512
    half = oa.shape[2]
    return pl.pallas_call(
        _outproj_kernel,
        grid=(B, S // tm),
        in_specs=[pl.BlockSpec((1, tm, half), lambda b, i: (b, i, 0)),
                  pl.BlockSpec((1, tm, half), lambda b, i: (b, i, 0)),
                  pl.BlockSpec((1, tm, D), lambda b, i: (b, i, 0)),
                  pl.BlockSpec((1, N_MOD, D), lambda b, i: (b, 0, 0)),
                  pl.BlockSpec((1, D), lambda b, i: (0, 0)),
                  pl.BlockSpec((2 * half, D), lambda b, i: (0, 0))],
        out_specs=[pl.BlockSpec((1, tm, D), lambda b, i: (b, i, 0)),
                   pl.BlockSpec((1, tm, D), lambda b, i: (b, i, 0))],
        out_shape=[jax.ShapeDtypeStruct((B, S, D), F32),
                   jax.ShapeDtypeStruct((B, S, D), BF16)],
        compiler_params=_cparams(("arbitrary", "arbitrary")),
        name="out_proj",
    )(oa, ob, x, mod, g_ffn.reshape(1, D), w_o)


def _ffn_kernel(h2_ref, x1_ref, mod_ref, wg_ref, wu_ref, cw_ref, cb_ref, wd_ref, gf_ref, o_ref,
                acc_ref, carry_ref, *, tm):
    i = pl.program_id(1)
    j = pl.program_id(2)
    nj = pl.num_programs(2)

    @pl.when(j == 0)
    def _():
        acc_ref[...] = jnp.zeros(acc_ref.shape, F32)

    h2 = h2_ref[0]
    g = jnp.dot(h2, wg_ref[...], preferred_element_type=F32)
    u = jnp.dot(h2, wu_ref[...], preferred_element_type=F32)

    prev = jnp.where(i > 0, carry_ref[j], 0.0)
    carry_ref[j] = g[tm - 8:, :]
    row = lax.broadcasted_iota(jnp.int32, g.shape, 0)
    g1 = jnp.where(row == 0, prev[7:8, :], pltpu.roll(g, 1, axis=0))
    g2 = jnp.where(row == 0, prev[6:7, :], jnp.where(row == 1, prev[7:8, :], pltpu.roll(g, 2, axis=0)))
    gc = cw_ref[0:1, :] * g2 + cw_ref[1:2, :] * g1 + cw_ref[2:3, :] * g + cb_ref[...]
    a = gc / (1.0 + jnp.exp(-gc)) * u
    acc_ref[...] += jnp.dot(a.astype(BF16), wd_ref[...], preferred_element_type=F32)

    @pl.when(j == nj - 1)
    def _():
        x2 = x1_ref[0] + mod_ref[0, 5:6, :] * acc_ref[...]
        ms = jnp.mean(x2 * x2, axis=-1, keepdims=True)
        o_ref[0] = x2 * lax.rsqrt(ms + EPS) * gf_ref[...]


def _conv_ffn(h2, x1, mod, w_gate, w_up, conv_w, conv_b, w_down, g_final):
    B, S, D = x1.shape
    F = w_gate.shape[1]
    tm, tf = 512, 512
    kern = functools.partial(_ffn_kernel, tm=tm)
    return pl.pallas_call(
        kern,
        grid=(B, S // tm, F // tf),
        in_specs=[pl.BlockSpec((1, tm, D), lambda b, i, j: (b, i, 0)),
                  pl.BlockSpec((1, tm, D), lambda b, i, j: (b, i, 0)),
                  pl.BlockSpec((1, N_MOD, D), lambda b, i, j: (b, 0, 0)),
                  pl.BlockSpec((D, tf), lambda b, i, j: (0, j)),
                  pl.BlockSpec((D, tf), lambda b, i, j: (0, j)),
                  pl.BlockSpec((3, tf), lambda b, i, j: (0, j)),
                  pl.BlockSpec((1, tf), lambda b, i, j: (0, j)),
                  pl.BlockSpec((tf, D), lambda b, i, j: (j, 0)),
                  pl.BlockSpec((1, D), lambda b, i, j: (0, 0))],
        out_specs=pl.BlockSpec((1, tm, D), lambda b, i, j: (b, i, 0)),
        out_shape=jax.ShapeDtypeStruct((B, S, D), F32),
        scratch_shapes=[pltpu.VMEM((tm, D), F32),
                        pltpu.VMEM((F // tf, 8, tf), F32)],
        compiler_params=_cparams(("arbitrary", "arbitrary", "arbitrary")),
        name="conv_ffn",
    )(h2, x1, mod, w_gate, w_up, conv_w, conv_b.reshape(1, F), w_down, g_final.reshape(1, D))


def _regroup_w_in(w_in):
    sizes = [1024, 256, 256, 1024, 64, 16, 1024, 256, 256]
    offs = [0]
    for s in sizes:
        offs.append(offs[-1] + s)
    qa, ka, va, qi, ki, wi, qb, kb, vb = [w_in[:, offs[n]:offs[n + 1]] for n in range(9)]
    z64 = jnp.zeros_like(ki)
    big = jnp.concatenate([qa, qi, qb, ka, va, kb, vb, ki, z64, z64, ki], axis=1).astype(BF16)
    tail = jnp.concatenate([wi, jnp.zeros((w_in.shape[0], 128 - IDX_HEADS), w_in.dtype)], axis=1).astype(BF16)
    return big, tail


def kernel(x, c, w_ada, b_ada, g_mix, w_in, sinks, w_o, g_ffn, w_gate, w_up, conv_w, conv_b, w_down, g_final):
    assert w_ada.shape[0] == 1, "the final norm is fused into the (single) layer's FFN kernel"
    mod = _ada_mod(c, w_ada[0], b_ada[0])
    w_big, w_tail = _regroup_w_in(w_in[0])
    proj, tail = _in_proj(x, mod, g_mix[0], w_big, w_tail)
    oa = _dsa_attention(proj, tail)
    ob = _swa_attention(proj, sinks[0])
    x1, h2 = _out_proj(oa, ob, x, mod, g_ffn[0], w_o[0].astype(BF16))
    return _conv_ffn(h2, x1, mod, w_gate[0].astype(BF16), w_up[0].astype(BF16),
                     conv_w[0], conv_b[0], w_down[0].astype(BF16), g_final)
```

```python
import functools

import jax
import jax.numpy as jnp
from jax import lax
from jax.experimental import pallas as pl
from jax.experimental.pallas import tpu as pltpu

F32 = jnp.float32
BF16 = jnp.bfloat16

D_MODEL = 2048
HEAD_DIM = 128
N_HEADS_A = 8
N_HEADS_B = 8
N_KV = 2
HEADS_PER_KV = 4
IDX_HEADS = 16
IDX_DIM = 64
TOPK_MAX = 256
WINDOW = 128
D_FF = 5632
N_MOD = 6
EPS = 1e-6
NEG = -1e30
ATTN_SCALE = HEAD_DIM ** -0.5
LOG2E = 1.4426950408889634
IDX_W_SCALE = (IDX_DIM ** -0.5) * (IDX_HEADS ** -0.5)
SLOPES = [2.0 ** (-8.0 * j / 16.0) for j in range(1, 17)]
SLOPES_A = SLOPES[0::2]
SLOPES_B = SLOPES[1::2]

COL_QA, COL_QI, COL_QB = 0, 1024, 2048
COL_KA, COL_VA, COL_KB, COL_VB = 3072, 3328, 3584, 3840
COL_KIA, COL_KIB = 4096, 4224
PROJ_W = 4352

VMEM_LIMIT = 56 * 1024 * 1024

KEY_LO0 = -2139095040
KEY_HI0 = 2139095040
F32_MIN_NORMAL = 1.1754943508222875e-38
F32_MIN_NORMAL_KEY = 0x00800000


def _cparams(sem):
    return pltpu.CompilerParams(dimension_semantics=sem, vmem_limit_bytes=VMEM_LIMIT)


def _ada_kernel(c_ref, w_ref, b_ref, o_ref):
    c = c_ref[...]
    s = c / (1.0 + jnp.exp(-c))
    o_ref[...] = jnp.dot(s.astype(BF16), w_ref[...].astype(BF16),
                         preferred_element_type=F32) + b_ref[...]


def _ada_mod(c, w_ada, b_ada):
    B, D = c.shape
    N = w_ada.shape[1]
    tn = 1024
    cp = jnp.zeros((8, D), F32).at[:B].set(c)
    out = pl.pallas_call(
        _ada_kernel,
        grid=(N // tn,),
        in_specs=[pl.BlockSpec((8, D), lambda j: (0, 0)),
                  pl.BlockSpec((D, tn), lambda j: (0, j)),
                  pl.BlockSpec((1, tn), lambda j: (0, j))],
        out_specs=pl.BlockSpec((8, tn), lambda j: (0, j)),
        out_shape=jax.ShapeDtypeStruct((8, N), F32),
        compiler_params=_cparams(("arbitrary",)),
        name="ada_mod",
    )(cp, w_ada, b_ada.reshape(1, N))
    return out[:B].reshape(B, N_MOD, D)


def _inproj_kernel(x_ref, mod_ref, g_ref, w_ref, wt_ref, p_ref, t_ref, *, chunk):
    x = x_ref[0]
    ms = jnp.mean(x * x, axis=-1, keepdims=True)
    y = x * lax.rsqrt(ms + EPS) * g_ref[...]
    h = y * (1.0 + mod_ref[0, 1:2, :]) + mod_ref[0, 0:1, :]
    hb = h.astype(BF16)
    for n in range(0, PROJ_W, chunk):
        p_ref[0, :, n:n + chunk] = jnp.dot(
            hb, w_ref[:, n:n + chunk], preferred_element_type=F32).astype(BF16)
    t_ref[0] = jnp.dot(hb, wt_ref[...], preferred_element_type=F32)


def _in_proj(x, mod, g_mix, w_big, w_tail):
    B, S, D = x.shape
    tm = 512
    return pl.pallas_call(
        functools.partial(_inproj_kernel, chunk=256),
        grid=(B, S // tm),
        in_specs=[pl.BlockSpec((1, tm, D), lambda b, i: (b, i, 0)),
                  pl.BlockSpec((1, N_MOD, D), lambda b, i: (b, 0, 0)),
                  pl.BlockSpec((1, D), lambda b, i: (0, 0)),
                  pl.BlockSpec((D, PROJ_W), lambda b, i: (0, 0)),
                  pl.BlockSpec((D, 128), lambda b, i: (0, 0))],
        out_specs=[pl.BlockSpec((1, tm, PROJ_W), lambda b, i: (b, i, 0)),
                   pl.BlockSpec((1, tm, 128), lambda b, i: (b, i, 0))],
        out_shape=[jax.ShapeDtypeStruct((B, S, PROJ_W), BF16),
                   jax.ShapeDtypeStruct((B, S, 128), F32)],
        compiler_params=_cparams(("arbitrary", "arbitrary")),
        name="in_proj",
    )(x, mod, g_mix.reshape(1, D), w_big, w_tail)


def _key_to_f32(k):
    bits = k ^ ((k >> 31) & 0x7FFFFFFF)
    return lax.bitcast_convert_type(bits, F32)


def _f32_to_key(f):
    bits = lax.bitcast_convert_type(f, jnp.int32)
    return bits ^ ((bits >> 31) & 0x7FFFFFFF)


def _dsa_kernel(qa_ref, qi_ref, w_ref, ka_ref, vat_ref, kia_ref, kib_ref, kpos_ref, qslope_ref, o_ref,
                score_ref, wt_ref, smin_ref, smax_ref, lm_ref, p_ref, mb_ref, m_ref, l_ref, acc_ref,
                *, TQ, TK, S, STEPS, CG):
    i = pl.program_id(1)
    q0 = i * TQ
    n_ch = i + 1
    kf = float(TOPK_MAX)
    dn_nt = (((1,), (1,)), ((), ()))

    key_i = lax.broadcasted_iota(jnp.int32, (TK, TQ), 0)
    qry_i = lax.broadcasted_iota(jnp.int32, (TK, TQ), 1)
    t_idx = q0 + lax.broadcasted_iota(jnp.int32, (1, TQ), 1)

    wt_ref[...] = w_ref[0].T * IDX_W_SCALE

    def chunk_steps(step):
        def group(p, carry):
            step(CG * p, CG)
            return carry

        lax.fori_loop(0, n_ch // CG, group, 0)
        size = CG // 2
        while size >= 1:
            @pl.when((n_ch & size) != 0)
            def _(size=size):
                step((n_ch // (2 * size)) * (2 * size), size)
            size //= 2

    smin_ref[...] = jnp.full(smin_ref.shape, jnp.inf, F32)
    smax_ref[...] = jnp.full(smax_ref.shape, -jnp.inf, F32)

    def idx_step(c0, nc):
        rows = nc * TK
        k0 = pl.multiple_of(c0 * TK, TK)
        kia = kia_ref[0, pl.ds(k0, rows), :]
        kib = kib_ref[0, pl.ds(k0, rows), :]
        acc = jnp.zeros((rows, TQ), F32)
        for j in range(IDX_HEADS // 2):
            qp = qi_ref[0, :, j * 128:(j + 1) * 128]
            sa = lax.dot_general(kia, qp, dn_nt, preferred_element_type=F32)
            sb = lax.dot_general(kib, qp, dn_nt, preferred_element_type=F32)
            acc = (acc + jnp.maximum(sa, 0.0) * wt_ref[2 * j:2 * j + 1, :]
                   + jnp.maximum(sb, 0.0) * wt_ref[2 * j + 1:2 * j + 2, :])
        key_r = lax.broadcasted_iota(jnp.int32, (rows, TQ), 0)
        qry_r = lax.broadcasted_iota(jnp.int32, (rows, TQ), 1)
        causal = (k0 + key_r) <= (q0 + qry_r)
        score_ref[pl.ds(c0, nc)] = jnp.where(causal, acc, -jnp.inf).reshape(nc, TK, TQ)
        smin_ref[...] = jnp.minimum(smin_ref[...],
                                    jnp.min(jnp.where(causal, acc, jnp.inf), axis=0, keepdims=True))
        smax_ref[...] = jnp.maximum(smax_ref[...],
                                    jnp.max(jnp.where(causal, acc, -jnp.inf), axis=0, keepdims=True))

    chunk_steps(idx_step)
    smin, smax = smin_ref[...], smax_ref[...]

    def count_keys(pred):
        def body(kc, acc):
            m = pred(score_ref[kc], kc)
            return acc + m.reshape(TK // 32, 4, 8, TQ).sum(axis=0)

        acc = lax.fori_loop(0, n_ch, body, jnp.zeros((4, 8, TQ), F32))
        return jnp.sum(acc.sum(axis=0), axis=0, keepdims=True)

    def count_ge(t):
        return count_keys(lambda blk, kc: jnp.where(blk >= t, 1.0, 0.0))

    n_masked = (S - 1 - t_idx).astype(F32)
    n_causal = (t_idx + 1).astype(F32)

    def bis_cond(st):
        return jnp.logical_and(st[4] > 0.5, st[5] < 2 * 32 + 8)

    def bis_body(st):
        lo, hi, cnt_lo, cph_lo, _, it = st
        for step in range(STEPS):
            mid = (lo >> 1) + (hi >> 1) + (lo & hi & 1)
            if step % 2 == 0:
                f_lo, f_hi = _key_to_f32(lo), _key_to_f32(hi)
                mid_v = _f32_to_key(f_lo + (f_hi - f_lo) * 0.5)
                mid = jnp.where(jnp.logical_and(mid_v > lo, mid_v < hi), mid_v, mid)
            fm = _key_to_f32(mid)
            cph = count_ge(fm)
            c = cph + jnp.where(fm <= NEG, n_masked, 0.0)
            ge = c >= kf
            lo = jnp.where(ge, mid, lo)
            hi = jnp.where(ge, hi, mid)
            cnt_lo = jnp.where(ge, c, cnt_lo)
            cph_lo = jnp.where(ge, cph, cph_lo)
        done = jnp.logical_or(cnt_lo == kf, lo + 1 >= hi)
        return lo, hi, cnt_lo, cph_lo, jnp.max(jnp.where(done, 0.0, 1.0)), it + STEPS

    below_neg = smin < NEG
    few = jnp.logical_and(n_causal < kf, jnp.logical_not(below_neg))
    key_neg = _f32_to_key(jnp.full((1, TQ), NEG, F32))
    lo0 = jnp.where(below_neg, KEY_LO0, jnp.where(few, key_neg, _f32_to_key(smin)))
    above_max = jnp.where(jnp.abs(smax) < F32_MIN_NORMAL, F32_MIN_NORMAL_KEY, _f32_to_key(smax) + 1)
    hi0 = jnp.where(below_neg, KEY_HI0, jnp.where(few, key_neg + 1, above_max))
    cnt0 = n_causal + jnp.where(_key_to_f32(lo0) <= NEG, n_masked, 0.0)
    st0 = (lo0, hi0, cnt0, n_causal, jnp.float32(1.0), jnp.int32(0))
    lo, hi, cnt_lo, cph_lo, _, _ = lax.while_loop(bis_cond, bis_body, st0)
    thr = _key_to_f32(lo)

    f_hi = _key_to_f32(hi)
    masked_hi = jnp.where(f_hi <= NEG, n_masked, 0.0)
    need = jnp.logical_and(cnt_lo > kf, cph_lo > kf - masked_hi)
    any_need = jnp.max(jnp.where(need, 1.0, 0.0)) > 0.5

    @pl.when(any_need)
    def _fix_ties():
        slots = kf - count_ge(f_hi) - masked_hi

        def jb(_, st):
            jlo, jhi = st
            jm = (jlo + jhi) >> 1
            ties = count_keys(lambda blk, kc: jnp.where(
                blk == thr, jnp.where(kc * TK + key_i <= jm, 1.0, 0.0), 0.0))
            ge = ties >= slots
            return jnp.where(ge, jlo, jm), jnp.where(ge, jm, jhi)

        n_iter = max(1, (S - 1).bit_length())
        _, jcut = lax.fori_loop(0, n_iter, jb, (jnp.full((1, TQ), -1, jnp.int32),
                                                 jnp.full((1, TQ), S - 1, jnp.int32)))
        jcut = jnp.where(need, jcut, S)

        def drop(kc, carry):
            blk = score_ref[kc]
            surplus = jnp.where(blk == thr, jnp.where(kc * TK + key_i > jcut, 1.0, 0.0), 0.0)
            score_ref[kc] = jnp.where(surplus > 0.5, -jnp.inf, blk)
            return carry

        lax.fori_loop(0, n_ch, drop, 0)

    m_ref[...] = jnp.full(m_ref.shape, NEG, F32)
    l_ref[...] = jnp.zeros(l_ref.shape, F32)
    acc_ref[...] = jnp.zeros(acc_ref.shape, F32)

    def att_step(c0, nc):
        rows = nc * TK
        k0 = pl.multiple_of(c0 * TK, TK)
        mb_ref[:rows] = jnp.where(score_ref[pl.ds(c0, nc)].reshape(rows, TQ) >= thr, 0.0, NEG)
        kpos = kpos_ref[pl.ds(k0, rows), :]
        for hh in range(N_HEADS_A):
            g = hh // HEADS_PER_KV
            kg = ka_ref[0, pl.ds(k0, rows), g * HEAD_DIM:(g + 1) * HEAD_DIM]
            qh = qa_ref[0, :, hh * HEAD_DIM:(hh + 1) * HEAD_DIM]
            lm_ref[hh, :rows] = lax.dot_general(jnp.concatenate([kg, kpos], axis=1),
                                                jnp.concatenate([qh, qslope_ref[hh]], axis=1),
                                                dn_nt, preferred_element_type=F32)
        for hh in range(N_HEADS_A):
            g = hh // HEADS_PER_KV
            vt = jnp.concatenate([vat_ref[0, c0 + c, g * HEAD_DIM:(g + 1) * HEAD_DIM, :] for c in range(nc)],
                                 axis=1)
            lm = lm_ref[hh, :rows] * (ATTN_SCALE * LOG2E) + mb_ref[:rows]
            lm_ref[hh, :rows] = lm
            m_old = m_ref[hh]
            m_new = jnp.maximum(m_old, jnp.max(lm, axis=0, keepdims=True))
            alpha = jnp.exp2(m_old - m_new)
            m_ref[hh] = m_new
            p = jnp.exp2(lm_ref[hh, :rows] - m_new)
            l_ref[hh] = alpha * l_ref[hh] + jnp.sum(p, axis=0, keepdims=True)
            p_ref[hh, :rows] = p.astype(BF16)
            pv = jnp.dot(vt, p_ref[hh, :rows], preferred_element_type=F32)
            acc_ref[hh] = alpha * acc_ref[hh] + pv

    chunk_steps(att_step)

    for hh in range(N_HEADS_A):
        o_t = acc_ref[hh] / l_ref[hh]
        o_ref[0, :, hh * HEAD_DIM:(hh + 1) * HEAD_DIM] = o_t.T.astype(BF16)


POS_RADIX = 64
N_SPLIT = 3


def _alibi_operands(n_pos, TQ, slopes):
    assert n_pos <= POS_RADIX * POS_RADIX
    s = jnp.arange(n_pos, dtype=jnp.int32)
    digits = [(s // POS_RADIX).astype(F32)] * N_SPLIT + [(s % POS_RADIX).astype(F32)] * N_SPLIT
    kpos = jnp.zeros((n_pos, 128), F32).at[:, :2 * N_SPLIT].set(jnp.stack(digits, axis=1)).astype(BF16)
    rows = []
    for slope in slopes:
        rest, pieces = jnp.float32(slope / ATTN_SCALE), []
        for _ in range(N_SPLIT):
            piece = rest.astype(BF16).astype(F32)
            pieces.append(piece)
            rest = rest - piece
        rows.append(jnp.stack([p * POS_RADIX for p in pieces] + pieces))
    qslope = jnp.zeros((len(slopes), 128), F32).at[:, :2 * N_SPLIT].set(jnp.stack(rows)).astype(BF16)
    return kpos, jnp.broadcast_to(qslope[:, None, :], (len(slopes), TQ, 128))


def _dsa_attention(proj, tail):
    B, S, _ = proj.shape
    TQ = TK = 256
    n_kc = S // TK
    vat = jnp.swapaxes(proj[:, :, COL_VA:COL_VA + N_KV * HEAD_DIM].reshape(B, n_kc, TK, N_KV * HEAD_DIM), 2, 3)
    kpos, qslope = _alibi_operands(S, TQ, SLOPES_A)
    CG = 2
    kern = functools.partial(_dsa_kernel, TQ=TQ, TK=TK, S=S, STEPS=2, CG=CG)
    return pl.pallas_call(
        kern,
        grid=(B, S // TQ),
        in_specs=[pl.BlockSpec((1, TQ, 1024), lambda b, i: (b, i, COL_QA // 1024)),
                  pl.BlockSpec((1, TQ, 1024), lambda b, i: (b, i, COL_QI // 1024)),
                  pl.BlockSpec((1, TQ, 128), lambda b, i: (b, i, 0)),
                  pl.BlockSpec((1, S, 256), lambda b, i: (b, 0, COL_KA // 256)),
                  pl.BlockSpec((1, n_kc, N_KV * HEAD_DIM, TK), lambda b, i: (b, 0, 0, 0)),
                  pl.BlockSpec((1, S, 128), lambda b, i: (b, 0, COL_KIA // 128)),
                  pl.BlockSpec((1, S, 128), lambda b, i: (b, 0, COL_KIB // 128)),
                  pl.BlockSpec((S, 128), lambda b, i: (0, 0)),
                  pl.BlockSpec((N_HEADS_A, TQ, 128), lambda b, i: (0, 0, 0))],
        out_specs=pl.BlockSpec((1, TQ, 1024), lambda b, i: (b, i, 0)),
        out_shape=jax.ShapeDtypeStruct((B, S, N_HEADS_A * HEAD_DIM), BF16),
        scratch_shapes=[pltpu.VMEM((n_kc, TK, TQ), F32),
                        pltpu.VMEM((128, TQ), F32),
                        pltpu.VMEM((1, TQ), F32),
                        pltpu.VMEM((1, TQ), F32),
                        pltpu.VMEM((N_HEADS_A, CG * TK, TQ), F32),
                        pltpu.VMEM((N_HEADS_A, CG * TK, TQ), BF16),
                        pltpu.VMEM((CG * TK, TQ), F32),
                        pltpu.VMEM((N_HEADS_A, 1, TQ), F32),
                        pltpu.VMEM((N_HEADS_A, 1, TQ), F32),
                        pltpu.VMEM((N_HEADS_A, HEAD_DIM, TQ), F32)],
        compiler_params=_cparams(("arbitrary", "arbitrary")),
        name="dsa_attn",
    )(proj, proj, tail, proj, vat, proj, proj, kpos, qslope)


def _swa_kernel(sink_ref, q_ref, k_ref, v_ref, kpos_ref, qslope_ref, o_ref, lm_ref, p_ref, mb_ref, *, TQ, KW):
    i = pl.program_id(1)
    q0 = i * TQ
    start = pl.multiple_of(jnp.maximum(q0 - WINDOW, 0), WINDOW)
    dn_nt = (((1,), (1,)), ((), ()))
    key_r = lax.broadcasted_iota(jnp.int32, (KW, TQ), 0)
    qry_r = lax.broadcasted_iota(jnp.int32, (KW, TQ), 1)
    dist = (q0 + qry_r) - (start + key_r)
    mb_ref[...] = jnp.where(dist >= 0, jnp.where(dist < WINDOW, 0.0, NEG), NEG)
    t_rel = (q0 - start + lax.broadcasted_iota(jnp.int32, (1, TQ), 1)).astype(F32)
    kpos = kpos_ref[...]
    for hh in range(N_HEADS_B):
        g = hh // HEADS_PER_KV
        kg = k_ref[0, pl.ds(start, KW), g * HEAD_DIM:(g + 1) * HEAD_DIM]
        qh = q_ref[0, :, hh * HEAD_DIM:(hh + 1) * HEAD_DIM]
        lm_ref[hh] = lax.dot_general(jnp.concatenate([kg, kpos], axis=1),
                                     jnp.concatenate([qh, qslope_ref[hh]], axis=1),
                                     dn_nt, preferred_element_type=F32)
    vts = [v_ref[0, pl.ds(start, KW), g * HEAD_DIM:(g + 1) * HEAD_DIM].astype(F32).T.astype(BF16)
           for g in range(N_KV)]
    for hh in range(N_HEADS_B):
        lm = lm_ref[hh] * (ATTN_SCALE * LOG2E) + mb_ref[...]
        sink = (sink_ref[hh] + SLOPES_B[hh] * t_rel) * LOG2E
        m = jnp.maximum(jnp.max(lm, axis=0, keepdims=True), sink)
        p = jnp.exp2(lm - m)
        den = jnp.sum(p, axis=0, keepdims=True) + jnp.exp2(sink - m)
        p_ref[hh] = p.astype(BF16)
        pv = jnp.dot(vts[hh // HEADS_PER_KV], p_ref[hh], preferred_element_type=F32)
        o_ref[0, :, hh * HEAD_DIM:(hh + 1) * HEAD_DIM] = (pv / den).T.astype(BF16)


def _swa_attention(proj, sinks):
    B, S, _ = proj.shape
    TQ = 256
    KW = TQ + WINDOW
    kpos, qslope = _alibi_operands(KW, TQ, SLOPES_B)
    kern = functools.partial(_swa_kernel, TQ=TQ, KW=KW)
    return pl.pallas_call(
        kern,
        grid=(B, S // TQ),
        in_specs=[pl.BlockSpec(memory_space=pltpu.SMEM),
                  pl.BlockSpec((1, TQ, 1024), lambda b, i: (b, i, COL_QB // 1024)),
                  pl.BlockSpec((1, S, 256), lambda b, i: (b, 0, COL_KB // 256)),
                  pl.BlockSpec((1, S, 256), lambda b, i: (b, 0, COL_VB // 256)),
                  pl.BlockSpec((KW, 128), lambda b, i: (0, 0)),
                  pl.BlockSpec((N_HEADS_B, TQ, 128), lambda b, i: (0, 0, 0))],
        out_specs=pl.BlockSpec((1, TQ, 1024), lambda b, i: (b, i, 0)),
        out_shape=jax.ShapeDtypeStruct((B, S, N_HEADS_B * HEAD_DIM), BF16),
        scratch_shapes=[pltpu.VMEM((N_HEADS_B, KW, TQ), F32),
                        pltpu.VMEM((N_HEADS_B, KW, TQ), BF16),
                        pltpu.VMEM((KW, TQ), F32)],
        compiler_params=_cparams(("arbitrary", "arbitrary")),
        name="swa_attn",
    )(sinks, proj, proj, proj, kpos, qslope)


def _outproj_kernel(oa_ref, ob_ref, x_ref, mod_ref, g_ref, w_ref, x1_ref, h2_ref):
    half = oa_ref.shape[2]
    mix = jnp.dot(oa_ref[0], w_ref[:half, :], preferred_element_type=F32)
    mix = mix + jnp.dot(ob_ref[0], w_ref[half:, :], preferred_element_type=F32)
    x1 = x_ref[0] + mod_ref[0, 2:3, :] * mix
    x1_ref[0] = x1
    ms = jnp.mean(x1 * x1, axis=-1, keepdims=True)
    y = x1 * lax.rsqrt(ms + EPS) * g_ref[...]
    h2_ref[0] = (y * (1.0 + mod_ref[0, 4:5, :]) + mod_ref[0, 3:4, :]).astype(BF16)


def _out_proj(oa, ob, x, mod, g_ffn, w_o):
    B, S, D = x.shape
    tm = 512
    half = oa.shape[2]
    return pl.pallas_call(
        _outproj_kernel,
        grid=(B, S // tm),
        in_specs=[pl.BlockSpec((1, tm, half), lambda b, i: (b, i, 0)),
                  pl.BlockSpec((1, tm, half), lambda b, i: (b, i, 0)),
                  pl.BlockSpec((1, tm, D), lambda b, i: (b, i, 0)),
                  pl.BlockSpec((1, N_MOD, D), lambda b, i: (b, 0, 0)),
                  pl.BlockSpec((1, D), lambda b, i: (0, 0)),
                  pl.BlockSpec((2 * half, D), lambda b, i: (0, 0))],
        out_specs=[pl.BlockSpec((1, tm, D), lambda b, i: (b, i, 0)),
                   pl.BlockSpec((1, tm, D), lambda b, i: (b, i, 0))],
        out_shape=[jax.ShapeDtypeStruct((B, S, D), F32),
                   jax.ShapeDtypeStruct((B, S, D), BF16)],
        compiler_params=_cparams(("arbitrary", "arbitrary")),
        name="out_proj",
    )(oa, ob, x, mod, g_ffn.reshape(1, D), w_o)


def _ffn_kernel(h2_ref, x1_ref, mod_ref, wg_ref, wu_ref, cw_ref, cb_ref, wd_ref, gf_ref, o_ref,
                acc_ref, carry_ref, *, tm):
    i = pl.program_id(1)
    j = pl.program_id(2)
    nj = pl.num_programs(2)

    @pl.when(j == 0)
    def _():
        acc_ref[...] = jnp.zeros(acc_ref.shape, F32)

    h2 = h2_ref[0]
    g = jnp.dot(h2, wg_ref[...], preferred_element_type=F32)
    u = jnp.dot(h2, wu_ref[...], preferred_element_type=F32)

    prev = jnp.where(i > 0, carry_ref[j], 0.0)
    carry_ref[j] = g[tm - 8:, :]
    row = lax.broadcasted_iota(jnp.int32, g.shape, 0)
    g1 = jnp.where(row == 0, prev[7:8, :], pltpu.roll(g, 1, axis=0))
    g2 = jnp.where(row == 0, prev[6:7, :], jnp.where(row == 1, prev[7:8, :], pltpu.roll(g, 2, axis=0)))
    gc = cw_ref[0:1, :] * g2 + cw_ref[1:2, :] * g1 + cw_ref[2:3, :] * g + cb_ref[...]
    a = gc / (1.0 + jnp.exp(-gc)) * u
    acc_ref[...] += jnp.dot(a.astype(BF16), wd_ref[...], preferred_element_type=F32)

    @pl.when(j == nj - 1)
    def _():
        x2 = x1_ref[0] + mod_ref[0, 5:6, :] * acc_ref[...]
        ms = jnp.mean(x2 * x2, axis=-1, keepdims=True)
        o_ref[0] = x2 * lax.rsqrt(ms + EPS) * gf_ref[...]


def _conv_ffn(h2, x1, mod, w_gate, w_up, conv_w, conv_b, w_down, g_final):
    B, S, D = x1.shape
    F = w_gate.shape[1]
    tm, tf = 512, 512
    kern = functools.partial(_ffn_kernel, tm=tm)
    return pl.pallas_call(
        kern,
        grid=(B, S // tm, F // tf),
        in_specs=[pl.BlockSpec((1, tm, D), lambda b, i, j: (b, i, 0)),
                  pl.BlockSpec((1, tm, D), lambda b, i, j: (b, i, 0)),
                  pl.BlockSpec((1, N_MOD, D), lambda b, i, j: (b, 0, 0)),
                  pl.BlockSpec((D, tf), lambda b, i, j: (0, j)),
                  pl.BlockSpec((D, tf), lambda b, i, j: (0, j)),
                  pl.BlockSpec((3, tf), lambda b, i, j: (0, j)),
                  pl.BlockSpec((1, tf), lambda b, i, j: (0, j)),
                  pl.BlockSpec((tf, D), lambda b, i, j: (j, 0)),
                  pl.BlockSpec((1, D), lambda b, i, j: (0, 0))],
        out_specs=pl.BlockSpec((1, tm, D), lambda b, i, j: (b, i, 0)),
        out_shape=jax.ShapeDtypeStruct((B, S, D), F32),
        scratch_shapes=[pltpu.VMEM((tm, D), F32),
                        pltpu.VMEM((F // tf, 8, tf), F32)],
        compiler_params=_cparams(("arbitrary", "arbitrary", "arbitrary")),
        name="conv_ffn",
    )(h2, x1, mod, w_gate, w_up, conv_w, conv_b.reshape(1, F), w_down, g_final.reshape(1, D))


def _regroup_w_in(w_in):
    sizes = [1024, 256, 256, 1024, 64, 16, 1024, 256, 256]
    offs = [0]
    for s in sizes:
        offs.append(offs[-1] + s)
    qa, ka, va, qi, ki, wi, qb, kb, vb = [w_in[:, offs[n]:offs[n + 1]] for n in range(9)]
    z64 = jnp.zeros_like(ki)
    big = jnp.concatenate([qa, qi, qb, ka, va, kb, vb, ki, z64, z64, ki], axis=1).astype(BF16)
    tail = jnp.concatenate([wi, jnp.zeros((w_in.shape[0], 128 - IDX_HEADS), w_in.dtype)], axis=1).astype(BF16)
    return big, tail


def kernel(x, c, w_ada, b_ada, g_mix, w_in, sinks, w_o, g_ffn, w_gate, w_up, conv_w, conv_b, w_down, g_final):
    assert w_ada.shape[0] == 1, "the final norm is fused into the (single) layer's FFN kernel"
    mod = _ada_mod(c, w_ada[0], b_ada[0])
    w_big, w_tail = _regroup_w_in(w_in[0])
    proj, tail = _in_proj(x, mod, g_mix[0], w_big, w_tail)
    oa = _dsa_attention(proj, tail)
    ob = _swa_attention(proj, sinks[0])
    x1, h2 = _out_proj(oa, ob, x, mod, g_ffn[0], w_o[0].astype(BF16))
    return _conv_ffn(h2, x1, mod, w_gate[0].astype(BF16), w_up[0].astype(BF16),
                     conv_w[0], conv_b[0], w_down[0].astype(BF16), g_final)
```

```python
import functools

import jax
import jax.numpy as jnp
from jax import lax
from jax.experimental import pallas as pl
from jax.experimental.pallas import tpu as pltpu

F32 = jnp.float32
BF16 = jnp.bfloat16

D_MODEL = 2048
HEAD_DIM = 128
N_HEADS_A = 8
N_HEADS_B = 8
N_KV = 2
HEADS_PER_KV = 4
IDX_HEADS = 16
IDX_DIM = 64
TOPK_MAX = 256
WINDOW = 128
D_FF = 5632
N_MOD = 6
EPS = 1e-6
NEG = -1e30
ATTN_SCALE = HEAD_DIM ** -0.5
LOG2E = 1.4426950408889634
IDX_W_SCALE = (IDX_DIM ** -0.5) * (IDX_HEADS ** -0.5)
SLOPES = [2.0 ** (-8.0 * j / 16.0) for j in range(1, 17)]
SLOPES_A = SLOPES[0::2]
SLOPES_B = SLOPES[1::2]

COL_QA, COL_QI, COL_QB = 0, 1024, 2048
COL_KA, COL_VA, COL_KB, COL_VB = 3072, 3328, 3584, 3840
COL_KIA, COL_KIB = 4096, 4224
PROJ_W = 4352

VMEM_LIMIT = 56 * 1024 * 1024

KEY_LO0 = -2139095040
KEY_HI0 = 2139095040
F32_MIN_NORMAL = 1.1754943508222875e-38
F32_MIN_NORMAL_KEY = 0x00800000


def _cparams(sem):
    return pltpu.CompilerParams(dimension_semantics=sem, vmem_limit_bytes=VMEM_LIMIT)


def _ada_kernel(c_ref, w_ref, b_ref, o_ref):
    c = c_ref[...]
    s = c / (1.0 + jnp.exp(-c))
    o_ref[...] = jnp.dot(s.astype(BF16), w_ref[...].astype(BF16),
                         preferred_element_type=F32) + b_ref[...]


def _ada_mod(c, w_ada, b_ada):
    B, D = c.shape
    N = w_ada.shape[1]
    tn = 1024
    cp = jnp.zeros((8, D), F32).at[:B].set(c)
    out = pl.pallas_call(
        _ada_kernel,
        grid=(N // tn,),
        in_specs=[pl.BlockSpec((8, D), lambda j: (0, 0)),
                  pl.BlockSpec((D, tn), lambda j: (0, j)),
                  pl.BlockSpec((1, tn), lambda j: (0, j))],
        out_specs=pl.BlockSpec((8, tn), lambda j: (0, j)),
        out_shape=jax.ShapeDtypeStruct((8, N), F32),
        compiler_params=_cparams(("arbitrary",)),
        name="ada_mod",
    )(cp, w_ada, b_ada.reshape(1, N))
    return out[:B].reshape(B, N_MOD, D)


def _inproj_kernel(x_ref, mod_ref, g_ref, w_ref, wt_ref, p_ref, t_ref, *, chunk):
    x = x_ref[0]
    ms = jnp.mean(x * x, axis=-1, keepdims=True)
    y = x * lax.rsqrt(ms + EPS) * g_ref[...]
    h = y * (1.0 + mod_ref[0, 1:2, :]) + mod_ref[0, 0:1, :]
    hb = h.astype(BF16)
    for n in range(0, PROJ_W, chunk):
        p_ref[0, :, n:n + chunk] = jnp.dot(
            hb, w_ref[:, n:n + chunk], preferred_element_type=F32).astype(BF16)
    t_ref[0] = jnp.dot(hb, wt_ref[...], preferred_element_type=F32)


def _in_proj(x, mod, g_mix, w_big, w_tail):
    B, S, D = x.shape
    tm = 512
    return pl.pallas_call(
        functools.partial(_inproj_kernel, chunk=256),
        grid=(B, S // tm),
        in_specs=[pl.BlockSpec((1, tm, D), lambda b, i: (b, i, 0)),
                  pl.BlockSpec((1, N_MOD, D), lambda b, i: (b, 0, 0)),
                  pl.BlockSpec((1, D), lambda b, i: (0, 0)),
                  pl.BlockSpec((D, PROJ_W), lambda b, i: (0, 0)),
                  pl.BlockSpec((D, 128), lambda b, i: (0, 0))],
        out_specs=[pl.BlockSpec((1, tm, PROJ_W), lambda b, i: (b, i, 0)),
                   pl.BlockSpec((1, tm, 128), lambda b, i: (b, i, 0))],
        out_shape=[jax.ShapeDtypeStruct((B, S, PROJ_W), BF16),
                   jax.ShapeDtypeStruct((B, S, 128), F32)],
        compiler_params=_cparams(("arbitrary", "arbitrary")),
        name="in_proj",
    )(x, mod, g_mix.reshape(1, D), w_big, w_tail)


def _key_to_f32(k):
    bits = k ^ ((k >> 31) & 0x7FFFFFFF)
    return lax.bitcast_convert_type(bits, F32)


def _f32_to_key(f):
    bits = lax.bitcast_convert_type(f, jnp.int32)
    return bits ^ ((bits >> 31) & 0x7FFFFFFF)


def _dsa_kernel(qa_ref, qi_ref, w_ref, ka_ref, va_ref, kia_ref, kib_ref, kpos_ref, qslope_ref, o_ref,
                vat_ref, score_ref, wt_ref, smin_ref, smax_ref, lm_ref, p_ref, mb_ref, m_ref, l_ref, acc_ref,
                *, TQ, TK, S, STEPS, CG, CG_IDX):
    i = pl.program_id(1)
    q0 = i * TQ
    n_ch = i + 1
    kf = float(TOPK_MAX)
    dn_nt = (((1,), (1,)), ((), ()))

    key_i = lax.broadcasted_iota(jnp.int32, (TK, TQ), 0)
    qry_i = lax.broadcasted_iota(jnp.int32, (TK, TQ), 1)
    t_idx = q0 + lax.broadcasted_iota(jnp.int32, (1, TQ), 1)

    wt_ref[...] = w_ref[0].T * IDX_W_SCALE

    @pl.when(i == 0)
    def _():
        def transpose_chunk(kc, carry):
            k0 = pl.multiple_of(kc * TK, TK)
            vat_ref[kc] = va_ref[0, pl.ds(k0, TK), :].astype(F32).T.astype(BF16)
            return carry

        lax.fori_loop(0, S // TK, transpose_chunk, 0)

    def chunk_steps(step, cg):
        def group(p, carry):
            step(cg * p, cg)
            return carry

        lax.fori_loop(0, n_ch // cg, group, 0)
        size = cg // 2
        while size >= 1:
            @pl.when((n_ch & size) != 0)
            def _(size=size):
                step((n_ch // (2 * size)) * (2 * size), size)
            size //= 2

    smin_ref[...] = jnp.full(smin_ref.shape, jnp.inf, F32)
    smax_ref[...] = jnp.full(smax_ref.shape, -jnp.inf, F32)

    def idx_step(c0, nc):
        rows = nc * TK
        k0 = pl.multiple_of(c0 * TK, TK)
        kia = kia_ref[0, pl.ds(k0, rows), :]
        kib = kib_ref[0, pl.ds(k0, rows), :]
        acc = jnp.zeros((rows, TQ), F32)
        for j in range(IDX_HEADS // 2):
            qp = qi_ref[0, :, j * 128:(j + 1) * 128]
            sa = lax.dot_general(kia, qp, dn_nt, preferred_element_type=F32)
            sb = lax.dot_general(kib, qp, dn_nt, preferred_element_type=F32)
            acc = (acc + jnp.maximum(sa, 0.0) * wt_ref[2 * j:2 * j + 1, :]
                   + jnp.maximum(sb, 0.0) * wt_ref[2 * j + 1:2 * j + 2, :])
        key_r = lax.broadcasted_iota(jnp.int32, (rows, TQ), 0)
        qry_r = lax.broadcasted_iota(jnp.int32, (rows, TQ), 1)
        causal = (k0 + key_r) <= (q0 + qry_r)
        score_ref[pl.ds(c0, nc)] = jnp.where(causal, acc, -jnp.inf).reshape(nc, TK, TQ)
        smin_ref[...] = jnp.minimum(smin_ref[...],
                                    jnp.min(jnp.where(causal, acc, jnp.inf), axis=0, keepdims=True))
        smax_ref[...] = jnp.maximum(smax_ref[...],
                                    jnp.max(jnp.where(causal, acc, -jnp.inf), axis=0, keepdims=True))

    chunk_steps(idx_step, CG_IDX)
    smin, smax = smin_ref[...], smax_ref[...]

    def count_keys(pred):
        def body(kc, acc):
            m = pred(score_ref[kc], kc)
            return acc + m.reshape(TK // 32, 4, 8, TQ).sum(axis=0)

        acc = lax.fori_loop(0, n_ch, body, jnp.zeros((4, 8, TQ), F32))
        return jnp.sum(acc.sum(axis=0), axis=0, keepdims=True)

    def count_ge(t):
        return count_keys(lambda blk, kc: jnp.where(blk >= t, 1.0, 0.0))

    n_masked = (S - 1 - t_idx).astype(F32)
    n_causal = (t_idx + 1).astype(F32)

    def bis_cond(st):
        return jnp.logical_and(st[4] > 0.5, st[5] < 2 * 32 + 8)

    def bis_body(st):
        lo, hi, cnt_lo, cph_lo, _, it = st
        for step in range(STEPS):
            mid = (lo >> 1) + (hi >> 1) + (lo & hi & 1)
            if step % 2 == 0:
                f_lo, f_hi = _key_to_f32(lo), _key_to_f32(hi)
                mid_v = _f32_to_key(f_lo + (f_hi - f_lo) * 0.5)
                mid = jnp.where(jnp.logical_and(mid_v > lo, mid_v < hi), mid_v, mid)
            fm = _key_to_f32(mid)
            cph = count_ge(fm)
            c = cph + jnp.where(fm <= NEG, n_masked, 0.0)
            ge = c >= kf
            lo = jnp.where(ge, mid, lo)
            hi = jnp.where(ge, hi, mid)
            cnt_lo = jnp.where(ge, c, cnt_lo)
            cph_lo = jnp.where(ge, cph, cph_lo)
        done = jnp.logical_or(cnt_lo == kf, lo + 1 >= hi)
        return lo, hi, cnt_lo, cph_lo, jnp.max(jnp.where(done, 0.0, 1.0)), it + STEPS

    below_neg = smin < NEG
    few = jnp.logical_and(n_causal < kf, jnp.logical_not(below_neg))
    key_neg = _f32_to_key(jnp.full((1, TQ), NEG, F32))
    lo0 = jnp.where(below_neg, KEY_LO0, jnp.where(few, key_neg, _f32_to_key(smin)))
    above_max = jnp.where(jnp.abs(smax) < F32_MIN_NORMAL, F32_MIN_NORMAL_KEY, _f32_to_key(smax) + 1)
    hi0 = jnp.where(below_neg, KEY_HI0, jnp.where(few, key_neg + 1, above_max))
    cnt0 = n_causal + jnp.where(_key_to_f32(lo0) <= NEG, n_masked, 0.0)
    st0 = (lo0, hi0, cnt0, n_causal, jnp.float32(1.0), jnp.int32(0))
    lo, hi, cnt_lo, cph_lo, _, _ = lax.while_loop(bis_cond, bis_body, st0)
    thr = _key_to_f32(lo)

    f_hi = _key_to_f32(hi)
    masked_hi = jnp.where(f_hi <= NEG, n_masked, 0.0)
    need = jnp.logical_and(cnt_lo > kf, cph_lo > kf - masked_hi)
    any_need = jnp.max(jnp.where(need, 1.0, 0.0)) > 0.5

    @pl.when(any_need)
    def _fix_ties():
        slots = kf - count_ge(f_hi) - masked_hi

        def jb(_, st):
            jlo, jhi = st
            jm = (jlo + jhi) >> 1
            ties = count_keys(lambda blk, kc: jnp.where(
                blk == thr, jnp.where(kc * TK + key_i <= jm, 1.0, 0.0), 0.0))
            ge = ties >= slots
            return jnp.where(ge, jlo, jm), jnp.where(ge, jm, jhi)

        n_iter = max(1, (S - 1).bit_length())
        _, jcut = lax.fori_loop(0, n_iter, jb, (jnp.full((1, TQ), -1, jnp.int32),
                                                 jnp.full((1, TQ), S - 1, jnp.int32)))
        jcut = jnp.where(need, jcut, S)

        def drop(kc, carry):
            blk = score_ref[kc]
            surplus = jnp.where(blk == thr, jnp.where(kc * TK + key_i > jcut, 1.0, 0.0), 0.0)
            score_ref[kc] = jnp.where(surplus > 0.5, -jnp.inf, blk)
            return carry

        lax.fori_loop(0, n_ch, drop, 0)

    m_ref[...] = jnp.full(m_ref.shape, NEG, F32)
    l_ref[...] = jnp.zeros(l_ref.shape, F32)
    acc_ref[...] = jnp.zeros(acc_ref.shape, F32)

    def att_step(c0, nc):
        rows = nc * TK
        k0 = pl.multiple_of(c0 * TK, TK)
        mb_ref[:rows] = jnp.where(score_ref[pl.ds(c0, nc)].reshape(rows, TQ) >= thr, 0.0, NEG)
        kpos = kpos_ref[pl.ds(k0, rows), :]
        col_max = []
        for hh in range(N_HEADS_A):
            g = hh // HEADS_PER_KV
            kg = ka_ref[0, pl.ds(k0, rows), g * HEAD_DIM:(g + 1) * HEAD_DIM]
            qh = qa_ref[0, :, hh * HEAD_DIM:(hh + 1) * HEAD_DIM]
            s = lax.dot_general(jnp.concatenate([kg, kpos], axis=1),
                                jnp.concatenate([qh, qslope_ref[hh]], axis=1),
                                dn_nt, preferred_element_type=F32)
            lm = s * (ATTN_SCALE * LOG2E) + mb_ref[:rows]
            lm_ref[hh, :rows] = lm
            col_max.append(jnp.max(lm, axis=0, keepdims=True))
        for hh in range(N_HEADS_A):
            g = hh // HEADS_PER_KV
            vt = jnp.concatenate([vat_ref[c0 + c, g * HEAD_DIM:(g + 1) * HEAD_DIM, :] for c in range(nc)],
                                 axis=1)
            m_old = m_ref[hh]
            m_new = jnp.maximum(m_old, col_max[hh])
            alpha = jnp.exp2(m_old - m_new)
            m_ref[hh] = m_new
            p = jnp.exp2(lm_ref[hh, :rows] - m_new)
            l_ref[hh] = alpha * l_ref[hh] + jnp.sum(p, axis=0, keepdims=True)
            p_ref[hh, :rows] = p.astype(BF16)
            pv = jnp.dot(vt, p_ref[hh, :rows], preferred_element_type=F32)
            acc_ref[hh] = alpha * acc_ref[hh] + pv

    chunk_steps(att_step, CG)

    for hh in range(N_HEADS_A):
        o_t = acc_ref[hh] / l_ref[hh]
        o_ref[0, :, hh * HEAD_DIM:(hh + 1) * HEAD_DIM] = o_t.T.astype(BF16)


POS_RADIX = 64
N_SPLIT = 3


def _alibi_operands(n_pos, TQ, slopes):
    assert n_pos <= POS_RADIX * POS_RADIX
    s = jnp.arange(n_pos, dtype=jnp.int32)
    digits = [(s // POS_RADIX).astype(F32)] * N_SPLIT + [(s % POS_RADIX).astype(F32)] * N_SPLIT
    kpos = jnp.zeros((n_pos, 128), F32).at[:, :2 * N_SPLIT].set(jnp.stack(digits, axis=1)).astype(BF16)
    rows = []
    for slope in slopes:
        rest, pieces = jnp.float32(slope / ATTN_SCALE), []
        for _ in range(N_SPLIT):
            piece = rest.astype(BF16).astype(F32)
            pieces.append(piece)
            rest = rest - piece
        rows.append(jnp.stack([p * POS_RADIX for p in pieces] + pieces))
    qslope = jnp.zeros((len(slopes), 128), F32).at[:, :2 * N_SPLIT].set(jnp.stack(rows)).astype(BF16)
    return kpos, jnp.broadcast_to(qslope[:, None, :], (len(slopes), TQ, 128))


def _dsa_attention(proj, tail):
    B, S, _ = proj.shape
    TQ = TK = 256
    n_kc = S // TK
    kpos, qslope = _alibi_operands(S, TQ, SLOPES_A)
    CG = 2
    CG_IDX = 4
    kern = functools.partial(_dsa_kernel, TQ=TQ, TK=TK, S=S, STEPS=2, CG=CG, CG_IDX=CG_IDX)
    return pl.pallas_call(
        kern,
        grid=(B, S // TQ),
        in_specs=[pl.BlockSpec((1, TQ, 1024), lambda b, i: (b, i, COL_QA // 1024)),
                  pl.BlockSpec((1, TQ, 1024), lambda b, i: (b, i, COL_QI // 1024)),
                  pl.BlockSpec((1, TQ, 128), lambda b, i: (b, i, 0)),
                  pl.BlockSpec((1, S, 256), lambda b, i: (b, 0, COL_KA // 256)),
                  pl.BlockSpec((1, S, 256), lambda b, i: (b, 0, COL_VA // 256)),
                  pl.BlockSpec((1, S, 128), lambda b, i: (b, 0, COL_KIA // 128)),
                  pl.BlockSpec((1, S, 128), lambda b, i: (b, 0, COL_KIB // 128)),
                  pl.BlockSpec((S, 128), lambda b, i: (0, 0)),
                  pl.BlockSpec((N_HEADS_A, TQ, 128), lambda b, i: (0, 0, 0))],
        out_specs=pl.BlockSpec((1, TQ, 1024), lambda b, i: (b, i, 0)),
        out_shape=jax.ShapeDtypeStruct((B, S, N_HEADS_A * HEAD_DIM), BF16),
        scratch_shapes=[pltpu.VMEM((n_kc, N_KV * HEAD_DIM, TK), BF16),
                        pltpu.VMEM((n_kc, TK, TQ), F32),
                        pltpu.VMEM((128, TQ), F32),
                        pltpu.VMEM((1, TQ), F32),
                        pltpu.VMEM((1, TQ), F32),
                        pltpu.VMEM((N_HEADS_A, CG * TK, TQ), F32),
                        pltpu.VMEM((N_HEADS_A, CG * TK, TQ), BF16),
                        pltpu.VMEM((CG * TK, TQ), F32),
                        pltpu.VMEM((N_HEADS_A, 1, TQ), F32),
                        pltpu.VMEM((N_HEADS_A, 1, TQ), F32),
                        pltpu.VMEM((N_HEADS_A, HEAD_DIM, TQ), F32)],
        compiler_params=_cparams(("arbitrary", "arbitrary")),
        name="dsa_attn",
    )(proj, proj, tail, proj, proj, proj, proj, kpos, qslope)


def _swa_kernel(sink_ref, q_ref, k_ref, v_ref, kpos_ref, qslope_ref, o_ref, lm_ref, p_ref, mb_ref, *, TQ, KW):
    i = pl.program_id(1)
    q0 = i * TQ
    start = pl.multiple_of(jnp.maximum(q0 - WINDOW, 0), WINDOW)
    dn_nt = (((1,), (1,)), ((), ()))
    key_r = lax.broadcasted_iota(jnp.int32, (KW, TQ), 0)
    qry_r = lax.broadcasted_iota(jnp.int32, (KW, TQ), 1)
    dist = (q0 + qry_r) - (start + key_r)
    mb_ref[...] = jnp.where(dist >= 0, jnp.where(dist < WINDOW, 0.0, NEG), NEG)
    t_rel = (q0 - start + lax.broadcasted_iota(jnp.int32, (1, TQ), 1)).astype(F32)
    kpos = kpos_ref[...]
    for hh in range(N_HEADS_B):
        g = hh // HEADS_PER_KV
        kg = k_ref[0, pl.ds(start, KW), g * HEAD_DIM:(g + 1) * HEAD_DIM]
        qh = q_ref[0, :, hh * HEAD_DIM:(hh + 1) * HEAD_DIM]
        lm_ref[hh] = lax.dot_general(jnp.concatenate([kg, kpos], axis=1),
                                     jnp.concatenate([qh, qslope_ref[hh]], axis=1),
                                     dn_nt, preferred_element_type=F32)
    vts = [v_ref[0, pl.ds(start, KW), g * HEAD_DIM:(g + 1) * HEAD_DIM].astype(F32).T.astype(BF16)
           for g in range(N_KV)]
    for hh in range(N_HEADS_B):
        lm = lm_ref[hh] * (ATTN_SCALE * LOG2E) + mb_ref[...]
        sink = (sink_ref[hh] + SLOPES_B[hh] * t_rel) * LOG2E
        m = jnp.maximum(jnp.max(lm, axis=0, keepdims=True), sink)
        p = jnp.exp2(lm - m)
        den = jnp.sum(p, axis=0, keepdims=True) + jnp.exp2(sink - m)
        p_ref[hh] = p.astype(BF16)
        pv = jnp.dot(vts[hh // HEADS_PER_KV], p_ref[hh], preferred_element_type=F32)
        o_ref[0, :, hh * HEAD_DIM:(hh + 1) * HEAD_DIM] = (pv / den).T.astype(BF16)


def _swa_attention(proj, sinks):
    B, S, _ = proj.shape
    TQ = 256
    KW = TQ + WINDOW
    kpos, qslope = _alibi_operands(KW, TQ, SLOPES_B)
    kern = functools.partial(_swa_kernel, TQ=TQ, KW=KW)
    return pl.pallas_call(
        kern,
        grid=(B, S // TQ),
        in_specs=[pl.BlockSpec(memory_space=pltpu.SMEM),
                  pl.BlockSpec((1, TQ, 1024), lambda b, i: (b, i, COL_QB // 1024)),
                  pl.BlockSpec((1, S, 256), lambda b, i: (b, 0, COL_KB // 256)),
                  pl.BlockSpec((1, S, 256), lambda b, i: (b, 0, COL_VB // 256)),
                  pl.BlockSpec((KW, 128), lambda b, i: (0, 0)),
                  pl.BlockSpec((N_HEADS_B, TQ, 128), lambda b, i: (0, 0, 0))],
        out_specs=pl.BlockSpec((1, TQ, 1024), lambda b, i: (b, i, 0)),
        out_shape=jax.ShapeDtypeStruct((B, S, N_HEADS_B * HEAD_DIM), BF16),
        scratch_shapes=[pltpu.VMEM((N_HEADS_B, KW, TQ), F32),
                        pltpu.VMEM((N_HEADS_B, KW, TQ), BF16),
                        pltpu.VMEM((KW, TQ), F32)],
        compiler_params=_cparams(("arbitrary", "arbitrary")),
        name="swa_attn",
    )(sinks, proj, proj, proj, kpos, qslope)


def _outproj_kernel(oa_ref, ob_ref, x_ref, mod_ref, g_ref, w_ref, x1_ref, h2_ref):
    half = oa_ref.shape[2]
    mix = jnp.dot(oa_ref[0], w_ref[:half, :], preferred_element_type=F32)
    mix = mix + jnp.dot(ob_ref[0], w_ref[half:, :], preferred_element_type=F32)
    x1 = x_ref[0] + mod_ref[0, 2:3, :] * mix
    x1_ref[0] = x1
    ms = jnp.mean(x1 * x1, axis=-1, keepdims=True)
    y = x1 * lax.rsqrt(ms + EPS) * g_ref[...]
    h2_ref[0] = (y * (1.0 + mod_ref[0, 4:5, :]) + mod_ref[0, 3:4, :]).astype(BF16)


def _out_proj(oa, ob, x, mod, g_ffn, w_o):
    B, S, D = x.shape
    tm = 512
    half = oa.shape[2]
    return pl.pallas_call(
        _outproj_kernel,
        grid=(B, S // tm),
        in_specs=[pl.BlockSpec((1, tm, half), lambda b, i: (b, i, 0)),
                  pl.BlockSpec((1, tm, half), lambda b, i: (b, i, 0)),
                  pl.BlockSpec((1, tm, D), lambda b, i: (b, i, 0)),
                  pl.BlockSpec((1, N_MOD, D), lambda b, i: (b, 0, 0)),
                  pl.BlockSpec((1, D), lambda b, i: (0, 0)),
                  pl.BlockSpec((2 * half, D), lambda b, i: (0, 0))],
        out_specs=[pl.BlockSpec((1, tm, D), lambda b, i: (b, i, 0)),
                   pl.BlockSpec((1, tm, D), lambda b, i: (b, i, 0))],
        out_shape=[jax.ShapeDtypeStruct((B, S, D), F32),
                   jax.ShapeDtypeStruct((B, S, D), BF16)],
        compiler_params=_cparams(("arbitrary", "arbitrary")),
        name="out_proj",
    )(oa, ob, x, mod, g_ffn.reshape(1, D), w_o)


def _ffn_kernel(h2_ref, x1_ref, mod_ref, wg_ref, wu_ref, cw_ref, cb_ref, wd_ref, gf_ref, o_ref,
                acc_ref, carry_ref, *, tm):
    i = pl.program_id(1)
    j = pl.program_id(2)
    nj = pl.num_programs(2)

    @pl.when(j == 0)
    def _():
        acc_ref[...] = jnp.zeros(acc_ref.shape, F32)

    h2 = h2_ref[0]
    g = jnp.dot(h2, wg_ref[...], preferred_element_type=F32)
    u = jnp.dot(h2, wu_ref[...], preferred_element_type=F32)

    prev = jnp.where(i > 0, carry_ref[j], 0.0)
    carry_ref[j] = g[tm - 8:, :]
    row = lax.broadcasted_iota(jnp.int32, g.shape, 0)
    g1 = jnp.where(row == 0, prev[7:8, :], pltpu.roll(g, 1, axis=0))
    g2 = jnp.where(row == 0, prev[6:7, :], jnp.where(row == 1, prev[7:8, :], pltpu.roll(g, 2, axis=0)))
    gc = cw_ref[0:1, :] * g2 + cw_ref[1:2, :] * g1 + cw_ref[2:3, :] * g + cb_ref[...]
    a = gc / (1.0 + jnp.exp(-gc)) * u
    acc_ref[...] += jnp.dot(a.astype(BF16), wd_ref[...], preferred_element_type=F32)

    @pl.when(j == nj - 1)
    def _():
        x2 = x1_ref[0] + mod_ref[0, 5:6, :] * acc_ref[...]
        ms = jnp.mean(x2 * x2, axis=-1, keepdims=True)
        o_ref[0] = x2 * lax.rsqrt(ms + EPS) * gf_ref[...]


def _conv_ffn(h2, x1, mod, w_gate, w_up, conv_w, conv_b, w_down, g_final):
    B, S, D = x1.shape
    F = w_gate.shape[1]
    tm, tf = 512, 512
    kern = functools.partial(_ffn_kernel, tm=tm)
    return pl.pallas_call(
        kern,
        grid=(B, S // tm, F // tf),
        in_specs=[pl.BlockSpec((1, tm, D), lambda b, i, j: (b, i, 0)),
                  pl.BlockSpec((1, tm, D), lambda b, i, j: (b, i, 0)),
                  pl.BlockSpec((1, N_MOD, D), lambda b, i, j: (b, 0, 0)),
                  pl.BlockSpec((D, tf), lambda b, i, j: (0, j)),
                  pl.BlockSpec((D, tf), lambda b, i, j: (0, j)),
                  pl.BlockSpec((3, tf), lambda b, i, j: (0, j)),
                  pl.BlockSpec((1, tf), lambda b, i, j: (0, j)),
                  pl.BlockSpec((tf, D), lambda b, i, j: (j, 0)),
                  pl.BlockSpec((1, D), lambda b, i, j: (0, 0))],
        out_specs=pl.BlockSpec((1, tm, D), lambda b, i, j: (b, i, 0)),
        out_shape=jax.ShapeDtypeStruct((B, S, D), F32),
        scratch_shapes=[pltpu.VMEM((tm, D), F32),
                        pltpu.VMEM((F // tf, 8, tf), F32)],
        compiler_params=_cparams(("arbitrary", "arbitrary", "arbitrary")),
        name="conv_ffn",
    )(h2, x1, mod, w_gate, w_up, conv_w, conv_b.reshape(1, F), w_down, g_final.reshape(1, D))


def _regroup_w_in(w_in):
    sizes = [1024, 256, 256, 1024, 64, 16, 1024, 256, 256]
    offs = [0]
    for s in sizes:
        offs.append(offs[-1] + s)
    qa, ka, va, qi, ki, wi, qb, kb, vb = [w_in[:, offs[n]:offs[n + 1]] for n in range(9)]
    z64 = jnp.zeros_like(ki)
    big = jnp.concatenate([qa, qi, qb, ka, va, kb, vb, ki, z64, z64, ki], axis=1).astype(BF16)
    tail = jnp.concatenate([wi, jnp.zeros((w_in.shape[0], 128 - IDX_HEADS), w_in.dtype)], axis=1).astype(BF16)
    return big, tail


def kernel(x, c, w_ada, b_ada, g_mix, w_in, sinks, w_o, g_ffn, w_gate, w_up, conv_w, conv_b, w_down, g_final):
    assert w_ada.shape[0] == 1, "the final norm is fused into the (single) layer's FFN kernel"
    mod = _ada_mod(c, w_ada[0], b_ada[0])
    w_big, w_tail = _regroup_w_in(w_in[0])
    proj, tail = _in_proj(x, mod, g_mix[0], w_big, w_tail)
    oa = _dsa_attention(proj, tail)
    ob = _swa_attention(proj, sinks[0])
    x1, h2 = _out_proj(oa, ob, x, mod, g_ffn[0], w_o[0].astype(BF16))
    return _conv_ffn(h2, x1, mod, w_gate[0].astype(BF16), w_up[0].astype(BF16),
                     conv_w[0], conv_b[0], w_down[0].astype(BF16), g_final)
```

```python
import functools

import jax
import jax.numpy as jnp
import numpy as np
from jax import lax
from jax.experimental import pallas as pl
from jax.experimental.pallas import tpu as pltpu

F32 = jnp.float32
BF16 = jnp.bfloat16

D_MODEL = 2048
HEAD_DIM = 128
N_HEADS_A = 8
N_HEADS_B = 8
N_KV = 2
HEADS_PER_KV = 4
IDX_HEADS = 16
IDX_DIM = 64
TOPK_MAX = 256
WINDOW = 128
D_FF = 5632
N_MOD = 6
EPS = 1e-6
NEG = -1e30
ATTN_SCALE = HEAD_DIM ** -0.5
LOG2E = 1.4426950408889634
IDX_W_SCALE = (IDX_DIM ** -0.5) * (IDX_HEADS ** -0.5)
SLOPES = [2.0 ** (-8.0 * j / 16.0) for j in range(1, 17)]
SLOPES_A = SLOPES[0::2]
SLOPES_B = SLOPES[1::2]

COL_QA, COL_QI, COL_QB = 0, 1024, 2048
COL_KA, COL_VA, COL_KB, COL_VB = 3072, 3328, 3584, 3840
COL_KIA, COL_KIB = 4096, 4224
PROJ_W = 4352

VMEM_LIMIT = 56 * 1024 * 1024

KEY_LO0 = -2139095040
KEY_HI0 = 2139095040
F32_MIN_NORMAL = 1.1754943508222875e-38
F32_MIN_NORMAL_KEY = 0x00800000


def _cparams(sem):
    return pltpu.CompilerParams(dimension_semantics=sem, vmem_limit_bytes=VMEM_LIMIT)


def _ada_kernel(c_ref, w_ref, b_ref, o_ref):
    c = c_ref[...]
    s = c / (1.0 + jnp.exp(-c))
    o_ref[...] = jnp.dot(s.astype(BF16), w_ref[...].astype(BF16),
                         preferred_element_type=F32) + b_ref[...]


def _ada_mod(c, w_ada, b_ada):
    B, D = c.shape
    N = w_ada.shape[1]
    tn = 1024
    cp = jnp.zeros((8, D), F32).at[:B].set(c)
    out = pl.pallas_call(
        _ada_kernel,
        grid=(N // tn,),
        in_specs=[pl.BlockSpec((8, D), lambda j: (0, 0)),
                  pl.BlockSpec((D, tn), lambda j: (0, j)),
                  pl.BlockSpec((1, tn), lambda j: (0, j))],
        out_specs=pl.BlockSpec((8, tn), lambda j: (0, j)),
        out_shape=jax.ShapeDtypeStruct((8, N), F32),
        compiler_params=_cparams(("arbitrary",)),
        name="ada_mod",
    )(cp, w_ada, b_ada.reshape(1, N))
    return out[:B].reshape(B, N_MOD, D)


def _inproj_kernel(x_ref, mod_ref, g_ref, w_ref, wt_ref, p_ref, t_ref, *, chunk):
    x = x_ref[0]
    ms = jnp.mean(x * x, axis=-1, keepdims=True)
    y = x * lax.rsqrt(ms + EPS) * g_ref[...]
    h = y * (1.0 + mod_ref[0, 1:2, :]) + mod_ref[0, 0:1, :]
    hb = h.astype(BF16)
    dn_nt = (((1,), (1,)), ((), ()))
    for n in range(0, PROJ_W, chunk):
        p_ref[0, :, n:n + chunk] = lax.dot_general(
            hb, w_ref[n:n + chunk, :], dn_nt, preferred_element_type=F32).astype(BF16)
    t_ref[0] = lax.dot_general(hb, wt_ref[...], dn_nt, preferred_element_type=F32)


def _in_proj(x, mod, g_mix, w_big, w_tail):
    B, S, D = x.shape
    tm = 512
    return pl.pallas_call(
        functools.partial(_inproj_kernel, chunk=256),
        grid=(B, S // tm),
        in_specs=[pl.BlockSpec((1, tm, D), lambda b, i: (b, i, 0)),
                  pl.BlockSpec((1, N_MOD, D), lambda b, i: (b, 0, 0)),
                  pl.BlockSpec((1, D), lambda b, i: (0, 0)),
                  pl.BlockSpec((PROJ_W, D), lambda b, i: (0, 0)),
                  pl.BlockSpec((128, D), lambda b, i: (0, 0))],
        out_specs=[pl.BlockSpec((1, tm, PROJ_W), lambda b, i: (b, i, 0)),
                   pl.BlockSpec((1, tm, 128), lambda b, i: (b, i, 0))],
        out_shape=[jax.ShapeDtypeStruct((B, S, PROJ_W), BF16),
                   jax.ShapeDtypeStruct((B, S, 128), F32)],
        compiler_params=_cparams(("arbitrary", "arbitrary")),
        name="in_proj",
    )(x, mod, g_mix.reshape(1, D), w_big, w_tail)


def _key_to_f32(k):
    bits = k ^ ((k >> 31) & 0x7FFFFFFF)
    return lax.bitcast_convert_type(bits, F32)


def _f32_to_key(f):
    bits = lax.bitcast_convert_type(f, jnp.int32)
    return bits ^ ((bits >> 31) & 0x7FFFFFFF)


def _dsa_kernel(qa_ref, qi_ref, w_ref, ka_ref, va_ref, kia_ref, kib_ref, kpos_ref, qslope_ref, o_ref,
                vat_ref, score_ref, wt_ref, smin_ref, smax_ref, lm_ref, p_ref, mb_ref, m_ref, l_ref, acc_ref,
                *, TQ, TK, S, STEPS, CG, CG_IDX):
    i = pl.program_id(1)
    q0 = i * TQ
    n_ch = i + 1
    kf = float(TOPK_MAX)
    dn_nt = (((1,), (1,)), ((), ()))

    key_i = lax.broadcasted_iota(jnp.int32, (TK, TQ), 0)
    qry_i = lax.broadcasted_iota(jnp.int32, (TK, TQ), 1)
    t_idx = q0 + lax.broadcasted_iota(jnp.int32, (1, TQ), 1)

    wt_ref[...] = w_ref[0].T * IDX_W_SCALE

    @pl.when(i == 0)
    def _():
        def transpose_chunk(kc, carry):
            k0 = pl.multiple_of(kc * TK, TK)
            vat_ref[kc] = va_ref[0, pl.ds(k0, TK), :].astype(F32).T.astype(BF16)
            return carry

        lax.fori_loop(0, S // TK, transpose_chunk, 0)

    def chunk_steps(step, cg):
        def group(p, carry):
            step(cg * p, cg)
            return carry

        lax.fori_loop(0, n_ch // cg, group, 0)
        size = cg // 2
        while size >= 1:
            @pl.when((n_ch & size) != 0)
            def _(size=size):
                step((n_ch // (2 * size)) * (2 * size), size)
            size //= 2

    smin_ref[...] = jnp.full(smin_ref.shape, jnp.inf, F32)
    smax_ref[...] = jnp.full(smax_ref.shape, -jnp.inf, F32)

    def idx_step(c0, nc):
        rows = nc * TK
        k0 = pl.multiple_of(c0 * TK, TK)
        kia = kia_ref[0, pl.ds(k0, rows), :]
        kib = kib_ref[0, pl.ds(k0, rows), :]
        acc = jnp.zeros((rows, TQ), F32)
        for j in range(IDX_HEADS // 2):
            qp = qi_ref[0, :, j * 128:(j + 1) * 128]
            sa = lax.dot_general(kia, qp, dn_nt, preferred_element_type=F32)
            sb = lax.dot_general(kib, qp, dn_nt, preferred_element_type=F32)
            acc = (acc + jnp.maximum(sa, 0.0) * wt_ref[2 * j:2 * j + 1, :]
                   + jnp.maximum(sb, 0.0) * wt_ref[2 * j + 1:2 * j + 2, :])
        key_r = lax.broadcasted_iota(jnp.int32, (rows, TQ), 0)
        qry_r = lax.broadcasted_iota(jnp.int32, (rows, TQ), 1)
        causal = (k0 + key_r) <= (q0 + qry_r)
        score_ref[pl.ds(c0, nc)] = jnp.where(causal, acc, -jnp.inf).reshape(nc, TK, TQ)
        smin_ref[...] = jnp.minimum(smin_ref[...],
                                    jnp.min(jnp.where(causal, acc, jnp.inf), axis=0, keepdims=True))
        smax_ref[...] = jnp.maximum(smax_ref[...],
                                    jnp.max(jnp.where(causal, acc, -jnp.inf), axis=0, keepdims=True))

    chunk_steps(idx_step, CG_IDX)
    smin, smax = smin_ref[...], smax_ref[...]

    def count_keys(pred):
        def body(kc, acc):
            m = pred(score_ref[kc], kc)
            return acc + m.reshape(TK // 32, 4, 8, TQ).sum(axis=0)

        acc = lax.fori_loop(0, n_ch, body, jnp.zeros((4, 8, TQ), F32))
        return jnp.sum(acc.sum(axis=0), axis=0, keepdims=True)

    def count_ge(t):
        return count_keys(lambda blk, kc: jnp.where(blk >= t, 1.0, 0.0))

    n_masked = (S - 1 - t_idx).astype(F32)
    n_causal = (t_idx + 1).astype(F32)

    def bis_cond(st):
        return jnp.logical_and(st[4] > 0.5, st[5] < 2 * 32 + 8)

    def bis_body(st):
        lo, hi, cnt_lo, cph_lo, _, it = st
        for step in range(STEPS):
            mid = (lo >> 1) + (hi >> 1) + (lo & hi & 1)
            if step % 2 == 0:
                f_lo, f_hi = _key_to_f32(lo), _key_to_f32(hi)
                mid_v = _f32_to_key(f_lo + (f_hi - f_lo) * 0.5)
                mid = jnp.where(jnp.logical_and(mid_v > lo, mid_v < hi), mid_v, mid)
            fm = _key_to_f32(mid)
            cph = count_ge(fm)
            c = cph + jnp.where(fm <= NEG, n_masked, 0.0)
            ge = c >= kf
            lo = jnp.where(ge, mid, lo)
            hi = jnp.where(ge, hi, mid)
            cnt_lo = jnp.where(ge, c, cnt_lo)
            cph_lo = jnp.where(ge, cph, cph_lo)
        done = jnp.logical_or(cnt_lo == kf, lo + 1 >= hi)
        return lo, hi, cnt_lo, cph_lo, jnp.max(jnp.where(done, 0.0, 1.0)), it + STEPS

    below_neg = smin < NEG
    few = jnp.logical_and(n_causal < kf, jnp.logical_not(below_neg))
    key_neg = _f32_to_key(jnp.full((1, TQ), NEG, F32))
    lo0 = jnp.where(below_neg, KEY_LO0, jnp.where(few, key_neg, _f32_to_key(smin)))
    above_max = jnp.where(jnp.abs(smax) < F32_MIN_NORMAL, F32_MIN_NORMAL_KEY, _f32_to_key(smax) + 1)
    hi0 = jnp.where(below_neg, KEY_HI0, jnp.where(few, key_neg + 1, above_max))
    cnt0 = n_causal + jnp.where(_key_to_f32(lo0) <= NEG, n_masked, 0.0)
    st0 = (lo0, hi0, cnt0, n_causal, jnp.float32(1.0), jnp.int32(0))
    lo, hi, cnt_lo, cph_lo, _, _ = lax.while_loop(bis_cond, bis_body, st0)
    thr = _key_to_f32(lo)

    f_hi = _key_to_f32(hi)
    masked_hi = jnp.where(f_hi <= NEG, n_masked, 0.0)
    need = jnp.logical_and(cnt_lo > kf, cph_lo > kf - masked_hi)
    any_need = jnp.max(jnp.where(need, 1.0, 0.0)) > 0.5

    @pl.when(any_need)
    def _fix_ties():
        slots = kf - count_ge(f_hi) - masked_hi

        def jb(_, st):
            jlo, jhi = st
            jm = (jlo + jhi) >> 1
            ties = count_keys(lambda blk, kc: jnp.where(
                blk == thr, jnp.where(kc * TK + key_i <= jm, 1.0, 0.0), 0.0))
            ge = ties >= slots
            return jnp.where(ge, jlo, jm), jnp.where(ge, jm, jhi)

        n_iter = max(1, (S - 1).bit_length())
        _, jcut = lax.fori_loop(0, n_iter, jb, (jnp.full((1, TQ), -1, jnp.int32),
                                                 jnp.full((1, TQ), S - 1, jnp.int32)))
        jcut = jnp.where(need, jcut, S)

        def drop(kc, carry):
            blk = score_ref[kc]
            surplus = jnp.where(blk == thr, jnp.where(kc * TK + key_i > jcut, 1.0, 0.0), 0.0)
            score_ref[kc] = jnp.where(surplus > 0.5, -jnp.inf, blk)
            return carry

        lax.fori_loop(0, n_ch, drop, 0)

    m_ref[...] = jnp.full(m_ref.shape, NEG, F32)
    l_ref[...] = jnp.zeros(l_ref.shape, F32)
    acc_ref[...] = jnp.zeros(acc_ref.shape, F32)

    def att_step(c0, nc):
        rows = nc * TK
        k0 = pl.multiple_of(c0 * TK, TK)
        mb_ref[:rows] = jnp.where(score_ref[pl.ds(c0, nc)].reshape(rows, TQ) >= thr, 0.0, NEG)
        kpos = kpos_ref[pl.ds(k0, rows), :]
        col_max = []
        for hh in range(N_HEADS_A):
            g = hh // HEADS_PER_KV
            kg = ka_ref[0, pl.ds(k0, rows), g * HEAD_DIM:(g + 1) * HEAD_DIM]
            qh = qa_ref[0, :, hh * HEAD_DIM:(hh + 1) * HEAD_DIM]
            s = lax.dot_general(jnp.concatenate([kg, kpos], axis=1),
                                jnp.concatenate([qh, qslope_ref[hh]], axis=1),
                                dn_nt, preferred_element_type=F32)
            lm = s * (ATTN_SCALE * LOG2E) + mb_ref[:rows]
            lm_ref[hh, :rows] = lm
            col_max.append(jnp.max(lm, axis=0, keepdims=True))
        for hh in range(N_HEADS_A):
            g = hh // HEADS_PER_KV
            vt = jnp.concatenate([vat_ref[c0 + c, g * HEAD_DIM:(g + 1) * HEAD_DIM, :] for c in range(nc)],
                                 axis=1)
            m_old = m_ref[hh]
            m_new = jnp.maximum(m_old, col_max[hh])
            alpha = jnp.exp2(m_old - m_new)
            m_ref[hh] = m_new
            p = jnp.exp2(lm_ref[hh, :rows] - m_new)
            l_ref[hh] = alpha * l_ref[hh] + jnp.sum(p, axis=0, keepdims=True)
            p_ref[hh, :rows] = p.astype(BF16)
            pv = jnp.dot(vt, p_ref[hh, :rows], preferred_element_type=F32)
            acc_ref[hh] = alpha * acc_ref[hh] + pv

    chunk_steps(att_step, CG)

    for hh in range(N_HEADS_A):
        o_t = acc_ref[hh] / l_ref[hh]
        o_ref[0, :, hh * HEAD_DIM:(hh + 1) * HEAD_DIM] = o_t.T.astype(BF16)


POS_RADIX = 64
N_SPLIT = 3


def _alibi_operands(n_pos, TQ, slopes):
    assert n_pos <= POS_RADIX * POS_RADIX
    s = np.arange(n_pos)
    kpos = np.zeros((n_pos, 128), np.float32)
    kpos[:, :N_SPLIT] = (s // POS_RADIX)[:, None]
    kpos[:, N_SPLIT:2 * N_SPLIT] = (s % POS_RADIX)[:, None]
    qslope = np.zeros((len(slopes), 128), np.float32)
    for h, slope in enumerate(slopes):
        rest = np.float32(slope / ATTN_SCALE)
        for n in range(N_SPLIT):
            piece = np.float32(rest.astype(BF16))
            qslope[h, n], qslope[h, N_SPLIT + n] = piece * POS_RADIX, piece
            rest = np.float32(rest - piece)
    qslope = np.broadcast_to(qslope[:, None, :], (len(slopes), TQ, 128))
    return jnp.asarray(kpos.astype(BF16)), jnp.asarray(qslope.astype(BF16))


def _dsa_attention(proj, tail):
    B, S, _ = proj.shape
    TQ = TK = 256
    n_kc = S // TK
    kpos, qslope = _alibi_operands(S, TQ, SLOPES_A)
    CG = 2
    CG_IDX = 4
    kern = functools.partial(_dsa_kernel, TQ=TQ, TK=TK, S=S, STEPS=2, CG=CG, CG_IDX=CG_IDX)
    return pl.pallas_call(
        kern,
        grid=(B, S // TQ),
        in_specs=[pl.BlockSpec((1, TQ, 1024), lambda b, i: (b, i, COL_QA // 1024)),
                  pl.BlockSpec((1, TQ, 1024), lambda b, i: (b, i, COL_QI // 1024)),
                  pl.BlockSpec((1, TQ, 128), lambda b, i: (b, i, 0)),
                  pl.BlockSpec((1, S, 256), lambda b, i: (b, 0, COL_KA // 256)),
                  pl.BlockSpec((1, S, 256), lambda b, i: (b, 0, COL_VA // 256)),
                  pl.BlockSpec((1, S, 128), lambda b, i: (b, 0, COL_KIA // 128)),
                  pl.BlockSpec((1, S, 128), lambda b, i: (b, 0, COL_KIB // 128)),
                  pl.BlockSpec((S, 128), lambda b, i: (0, 0)),
                  pl.BlockSpec((N_HEADS_A, TQ, 128), lambda b, i: (0, 0, 0))],
        out_specs=pl.BlockSpec((1, TQ, 1024), lambda b, i: (b, i, 0)),
        out_shape=jax.ShapeDtypeStruct((B, S, N_HEADS_A * HEAD_DIM), BF16),
        scratch_shapes=[pltpu.VMEM((n_kc, N_KV * HEAD_DIM, TK), BF16),
                        pltpu.VMEM((n_kc, TK, TQ), F32),
                        pltpu.VMEM((128, TQ), F32),
                        pltpu.VMEM((1, TQ), F32),
                        pltpu.VMEM((1, TQ), F32),
                        pltpu.VMEM((N_HEADS_A, CG * TK, TQ), F32),
                        pltpu.VMEM((N_HEADS_A, CG * TK, TQ), BF16),
                        pltpu.VMEM((CG * TK, TQ), F32),
                        pltpu.VMEM((N_HEADS_A, 1, TQ), F32),
                        pltpu.VMEM((N_HEADS_A, 1, TQ), F32),
                        pltpu.VMEM((N_HEADS_A, HEAD_DIM, TQ), F32)],
        compiler_params=_cparams(("arbitrary", "arbitrary")),
        name="dsa_attn",
    )(proj, proj, tail, proj, proj, proj, proj, kpos, qslope)


def _swa_kernel(sink_ref, q_ref, k_ref, v_ref, kpos_ref, qslope_ref, o_ref, lm_ref, p_ref, mb_ref, *, TQ, KW):
    i = pl.program_id(1)
    q0 = i * TQ
    start = pl.multiple_of(jnp.maximum(q0 - WINDOW, 0), WINDOW)
    dn_nt = (((1,), (1,)), ((), ()))
    key_r = lax.broadcasted_iota(jnp.int32, (KW, TQ), 0)
    qry_r = lax.broadcasted_iota(jnp.int32, (KW, TQ), 1)
    dist = (q0 + qry_r) - (start + key_r)
    mb_ref[...] = jnp.where(dist >= 0, jnp.where(dist < WINDOW, 0.0, NEG), NEG)
    t_rel = (q0 - start + lax.broadcasted_iota(jnp.int32, (1, TQ), 1)).astype(F32)
    kpos = kpos_ref[...]
    for hh in range(N_HEADS_B):
        g = hh // HEADS_PER_KV
        kg = k_ref[0, pl.ds(start, KW), g * HEAD_DIM:(g + 1) * HEAD_DIM]
        qh = q_ref[0, :, hh * HEAD_DIM:(hh + 1) * HEAD_DIM]
        lm_ref[hh] = lax.dot_general(jnp.concatenate([kg, kpos], axis=1),
                                     jnp.concatenate([qh, qslope_ref[hh]], axis=1),
                                     dn_nt, preferred_element_type=F32)
    vts = [v_ref[0, pl.ds(start, KW), g * HEAD_DIM:(g + 1) * HEAD_DIM].astype(F32).T.astype(BF16)
           for g in range(N_KV)]
    for hh in range(N_HEADS_B):
        lm = lm_ref[hh] * (ATTN_SCALE * LOG2E) + mb_ref[...]
        sink = (sink_ref[hh] + SLOPES_B[hh] * t_rel) * LOG2E
        m = jnp.maximum(jnp.max(lm, axis=0, keepdims=True), sink)
        p = jnp.exp2(lm - m)
        den = jnp.sum(p, axis=0, keepdims=True) + jnp.exp2(sink - m)
        p_ref[hh] = p.astype(BF16)
        pv = jnp.dot(vts[hh // HEADS_PER_KV], p_ref[hh], preferred_element_type=F32)
        o_ref[0, :, hh * HEAD_DIM:(hh + 1) * HEAD_DIM] = (pv / den).T.astype(BF16)


def _swa_attention(proj, sinks):
    B, S, _ = proj.shape
    TQ = 256
    KW = TQ + WINDOW
    kpos, qslope = _alibi_operands(KW, TQ, SLOPES_B)
    kern = functools.partial(_swa_kernel, TQ=TQ, KW=KW)
    return pl.pallas_call(
        kern,
        grid=(B, S // TQ),
        in_specs=[pl.BlockSpec(memory_space=pltpu.SMEM),
                  pl.BlockSpec((1, TQ, 1024), lambda b, i: (b, i, COL_QB // 1024)),
                  pl.BlockSpec((1, S, 256), lambda b, i: (b, 0, COL_KB // 256)),
                  pl.BlockSpec((1, S, 256), lambda b, i: (b, 0, COL_VB // 256)),
                  pl.BlockSpec((KW, 128), lambda b, i: (0, 0)),
                  pl.BlockSpec((N_HEADS_B, TQ, 128), lambda b, i: (0, 0, 0))],
        out_specs=pl.BlockSpec((1, TQ, 1024), lambda b, i: (b, i, 0)),
        out_shape=jax.ShapeDtypeStruct((B, S, N_HEADS_B * HEAD_DIM), BF16),
        scratch_shapes=[pltpu.VMEM((N_HEADS_B, KW, TQ), F32),
                        pltpu.VMEM((N_HEADS_B, KW, TQ), BF16),
                        pltpu.VMEM((KW, TQ), F32)],
        compiler_params=_cparams(("arbitrary", "arbitrary")),
        name="swa_attn",
    )(sinks, proj, proj, proj, kpos, qslope)


def _outproj_kernel(oa_ref, ob_ref, x_ref, mod_ref, g_ref, w_ref, x1_ref, h2_ref):
    half = oa_ref.shape[2]
    mix = jnp.dot(oa_ref[0], w_ref[:half, :], preferred_element_type=F32)
    mix = mix + jnp.dot(ob_ref[0], w_ref[half:, :], preferred_element_type=F32)
    x1 = x_ref[0] + mod_ref[0, 2:3, :] * mix
    x1_ref[0] = x1
    ms = jnp.mean(x1 * x1, axis=-1, keepdims=True)
    y = x1 * lax.rsqrt(ms + EPS) * g_ref[...]
    h2_ref[0] = (y * (1.0 + mod_ref[0, 4:5, :]) + mod_ref[0, 3:4, :]).astype(BF16)


def _out_proj(oa, ob, x, mod, g_ffn, w_o):
    B, S, D = x.shape
    tm = 512
    half = oa.shape[2]
    return pl.pallas_call(
        _outproj_kernel,
        grid=(B, S // tm),
        in_specs=[pl.BlockSpec((1, tm, half), lambda b, i: (b, i, 0)),
                  pl.BlockSpec((1, tm, half), lambda b, i: (b, i, 0)),
                  pl.BlockSpec((1, tm, D), lambda b, i: (b, i, 0)),
                  pl.BlockSpec((1, N_MOD, D), lambda b, i: (b, 0, 0)),
                  pl.BlockSpec((1, D), lambda b, i: (0, 0)),
                  pl.BlockSpec((2 * half, D), lambda b, i: (0, 0))],
        out_specs=[pl.BlockSpec((1, tm, D), lambda b, i: (b, i, 0)),
                   pl.BlockSpec((1, tm, D), lambda b, i: (b, i, 0))],
        out_shape=[jax.ShapeDtypeStruct((B, S, D), F32),
                   jax.ShapeDtypeStruct((B, S, D), BF16)],
        compiler_params=_cparams(("arbitrary", "arbitrary")),
        name="out_proj",
    )(oa, ob, x, mod, g_ffn.reshape(1, D), w_o)


def _ffn_kernel(h2_ref, x1_ref, mod_ref, wg_ref, wu_ref, cw_ref, cb_ref, wd_ref, gf_ref, o_ref,
                acc_ref, carry_ref, *, tm):
    i = pl.program_id(1)
    j = pl.program_id(2)
    nj = pl.num_programs(2)

    @pl.when(j == 0)
    def _():
        acc_ref[...] = jnp.zeros(acc_ref.shape, F32)

    h2 = h2_ref[0]
    g = jnp.dot(h2, wg_ref[...], preferred_element_type=F32)
    u = jnp.dot(h2, wu_ref[...], preferred_element_type=F32)

    prev = jnp.where(i > 0, carry_ref[j], 0.0)
    carry_ref[j] = g[tm - 8:, :]
    row = lax.broadcasted_iota(jnp.int32, g.shape, 0)
    g1 = jnp.where(row == 0, prev[7:8, :], pltpu.roll(g, 1, axis=0))
    g2 = jnp.where(row == 0, prev[6:7, :], jnp.where(row == 1, prev[7:8, :], pltpu.roll(g, 2, axis=0)))
    gc = cw_ref[0:1, :] * g2 + cw_ref[1:2, :] * g1 + cw_ref[2:3, :] * g + cb_ref[...]
    a = gc / (1.0 + jnp.exp(-gc)) * u
    acc_ref[...] += jnp.dot(a.astype(BF16), wd_ref[...], preferred_element_type=F32)

    @pl.when(j == nj - 1)
    def _():
        x2 = x1_ref[0] + mod_ref[0, 5:6, :] * acc_ref[...]
        ms = jnp.mean(x2 * x2, axis=-1, keepdims=True)
        o_ref[0] = x2 * lax.rsqrt(ms + EPS) * gf_ref[...]


def _conv_ffn(h2, x1, mod, w_gate, w_up, conv_w, conv_b, w_down, g_final):
    B, S, D = x1.shape
    F = w_gate.shape[1]
    tm, tf = 512, 512
    kern = functools.partial(_ffn_kernel, tm=tm)
    return pl.pallas_call(
        kern,
        grid=(B, S // tm, F // tf),
        in_specs=[pl.BlockSpec((1, tm, D), lambda b, i, j: (b, i, 0)),
                  pl.BlockSpec((1, tm, D), lambda b, i, j: (b, i, 0)),
                  pl.BlockSpec((1, N_MOD, D), lambda b, i, j: (b, 0, 0)),
                  pl.BlockSpec((D, tf), lambda b, i, j: (0, j)),
                  pl.BlockSpec((D, tf), lambda b, i, j: (0, j)),
                  pl.BlockSpec((3, tf), lambda b, i, j: (0, j)),
                  pl.BlockSpec((1, tf), lambda b, i, j: (0, j)),
                  pl.BlockSpec((tf, D), lambda b, i, j: (j, 0)),
                  pl.BlockSpec((1, D), lambda b, i, j: (0, 0))],
        out_specs=pl.BlockSpec((1, tm, D), lambda b, i, j: (b, i, 0)),
        out_shape=jax.ShapeDtypeStruct((B, S, D), F32),
        scratch_shapes=[pltpu.VMEM((tm, D), F32),
                        pltpu.VMEM((F // tf, 8, tf), F32)],
        compiler_params=_cparams(("arbitrary", "arbitrary", "arbitrary")),
        name="conv_ffn",
    )(h2, x1, mod, w_gate, w_up, conv_w, conv_b.reshape(1, F), w_down, g_final.reshape(1, D))


def _regroup_w_in(w_in):
    sizes = [1024, 256, 256, 1024, 64, 16, 1024, 256, 256]
    offs = [0]
    for s in sizes:
        offs.append(offs[-1] + s)
    w_t = w_in.T
    qa, ka, va, qi, ki, wi, qb, kb, vb = [w_t[offs[n]:offs[n + 1]] for n in range(9)]
    z64 = jnp.zeros_like(ki)
    big = jnp.concatenate([qa, qi, qb, ka, va, kb, vb, ki, z64, z64, ki], axis=0).astype(BF16)
    tail = jnp.concatenate([wi, jnp.zeros((128 - IDX_HEADS, w_in.shape[0]), w_in.dtype)], axis=0).astype(BF16)
    return big, tail


def kernel(x, c, w_ada, b_ada, g_mix, w_in, sinks, w_o, g_ffn, w_gate, w_up, conv_w, conv_b, w_down, g_final):
    assert w_ada.shape[0] == 1, "the final norm is fused into the (single) layer's FFN kernel"
    mod = _ada_mod(c, w_ada[0], b_ada[0])
    w_big, w_tail = _regroup_w_in(w_in[0])
    proj, tail = _in_proj(x, mod, g_mix[0], w_big, w_tail)
    oa = _dsa_attention(proj, tail)
    ob = _swa_attention(proj, sinks[0])
    x1, h2 = _out_proj(oa, ob, x, mod, g_ffn[0], w_o[0].astype(BF16))
    return _conv_ffn(h2, x1, mod, w_gate[0].astype(BF16), w_up[0].astype(BF16),
                     conv_w[0], conv_b[0], w_down[0].astype(BF16), g_final)
```

```python
import functools

import jax
import jax.numpy as jnp
import numpy as np
from jax import lax
from jax.experimental import pallas as pl
from jax.experimental.pallas import tpu as pltpu

F32 = jnp.float32
BF16 = jnp.bfloat16

D_MODEL = 2048
HEAD_DIM = 128
N_HEADS_A = 8
N_HEADS_B = 8
N_KV = 2
HEADS_PER_KV = 4
IDX_HEADS = 16
IDX_DIM = 64
TOPK_MAX = 256
WINDOW = 128
D_FF = 5632
N_MOD = 6
EPS = 1e-6
NEG = -1e30
ATTN_SCALE = HEAD_DIM ** -0.5
LOG2E = 1.4426950408889634
IDX_W_SCALE = (IDX_DIM ** -0.5) * (IDX_HEADS ** -0.5)
SLOPES = [2.0 ** (-8.0 * j / 16.0) for j in range(1, 17)]
SLOPES_A = SLOPES[0::2]
SLOPES_B = SLOPES[1::2]

COL_QA, COL_QI, COL_QB = 0, 1024, 2048
COL_KA, COL_VA, COL_KB, COL_VB = 3072, 3328, 3584, 3840
COL_KIA, COL_KIB = 4096, 4224
PROJ_W = 4352

VMEM_LIMIT = 56 * 1024 * 1024

KEY_LO0 = -2139095040
KEY_HI0 = 2139095040
F32_MIN_NORMAL = 1.1754943508222875e-38
F32_MIN_NORMAL_KEY = 0x00800000


def _cparams(sem):
    return pltpu.CompilerParams(dimension_semantics=sem, vmem_limit_bytes=VMEM_LIMIT)


def _ada_kernel(c_ref, w_ref, b_ref, o_ref):
    c = c_ref[...]
    s = c / (1.0 + jnp.exp(-c))
    o_ref[...] = jnp.dot(s.astype(BF16), w_ref[...].astype(BF16),
                         preferred_element_type=F32) + b_ref[...]


def _ada_mod(c, w_ada, b_ada):
    B, D = c.shape
    N = w_ada.shape[1]
    tn = 1024
    cp = jnp.zeros((8, D), F32).at[:B].set(c)
    out = pl.pallas_call(
        _ada_kernel,
        grid=(N // tn,),
        in_specs=[pl.BlockSpec((8, D), lambda j: (0, 0)),
                  pl.BlockSpec((D, tn), lambda j: (0, j)),
                  pl.BlockSpec((1, tn), lambda j: (0, j))],
        out_specs=pl.BlockSpec((8, tn), lambda j: (0, j)),
        out_shape=jax.ShapeDtypeStruct((8, N), F32),
        compiler_params=_cparams(("arbitrary",)),
        name="ada_mod",
    )(cp, w_ada, b_ada.reshape(1, N))
    return out[:B].reshape(B, N_MOD, D)


def _inproj_kernel(x_ref, mod_ref, g_ref, w_ref, wt_ref, *rest, chunk, n_cast):
    cast_in, (p_ref, t_ref), cast_out = rest[:n_cast], rest[n_cast:n_cast + 2], rest[n_cast + 2:]
    for src, dst in zip(cast_in, cast_out):
        dst[...] = src[...].astype(BF16)
    x = x_ref[0]
    ms = jnp.mean(x * x, axis=-1, keepdims=True)
    y = x * lax.rsqrt(ms + EPS) * g_ref[...]
    h = y * (1.0 + mod_ref[0, 1:2, :]) + mod_ref[0, 0:1, :]
    hb = h.astype(BF16)
    dn_nt = (((1,), (1,)), ((), ()))
    for n in range(0, PROJ_W, chunk):
        p_ref[0, :, n:n + chunk] = lax.dot_general(
            hb, w_ref[n:n + chunk, :], dn_nt, preferred_element_type=F32).astype(BF16)
    t_ref[0] = lax.dot_general(hb, wt_ref[...], dn_nt, preferred_element_type=F32)


def _in_proj(x, mod, g_mix, w_big, w_tail, f32_weights):
    B, S, D = x.shape
    tm = 512
    n_steps = B * (S // tm)
    cast_specs = []
    for w in f32_weights:
        rows = w.shape[0] // n_steps
        assert rows * n_steps == w.shape[0] and rows % 16 == 0, w.shape
        cast_specs.append(pl.BlockSpec((rows, w.shape[1]), lambda b, i: (b * (S // tm) + i, 0)))
    outs = pl.pallas_call(
        functools.partial(_inproj_kernel, chunk=256, n_cast=len(f32_weights)),
        grid=(B, S // tm),
        in_specs=[pl.BlockSpec((1, tm, D), lambda b, i: (b, i, 0)),
                  pl.BlockSpec((1, N_MOD, D), lambda b, i: (b, 0, 0)),
                  pl.BlockSpec((1, D), lambda b, i: (0, 0)),
                  pl.BlockSpec((PROJ_W, D), lambda b, i: (0, 0)),
                  pl.BlockSpec((128, D), lambda b, i: (0, 0))] + cast_specs,
        out_specs=[pl.BlockSpec((1, tm, PROJ_W), lambda b, i: (b, i, 0)),
                   pl.BlockSpec((1, tm, 128), lambda b, i: (b, i, 0))] + cast_specs,
        out_shape=[jax.ShapeDtypeStruct((B, S, PROJ_W), BF16),
                   jax.ShapeDtypeStruct((B, S, 128), F32)]
                  + [jax.ShapeDtypeStruct(w.shape, BF16) for w in f32_weights],
        compiler_params=_cparams(("arbitrary", "arbitrary")),
        name="in_proj",
    )(x, mod, g_mix.reshape(1, D), w_big, w_tail, *f32_weights)
    return outs[0], outs[1], outs[2:]


def _key_to_f32(k):
    bits = k ^ ((k >> 31) & 0x7FFFFFFF)
    return lax.bitcast_convert_type(bits, F32)


def _f32_to_key(f):
    bits = lax.bitcast_convert_type(f, jnp.int32)
    return bits ^ ((bits >> 31) & 0x7FFFFFFF)


def _dsa_kernel(qa_ref, qi_ref, w_ref, ka_ref, va_ref, kia_ref, kib_ref, kpos_ref, qslope_ref, o_ref,
                vat_ref, score_ref, wt_ref, smin_ref, smax_ref, lm_ref, p_ref, mb_ref, m_ref, l_ref, acc_ref,
                *, TQ, TK, S, STEPS, CG, CG_IDX):
    i = pl.program_id(1)
    q0 = i * TQ
    n_ch = i + 1
    kf = float(TOPK_MAX)
    dn_nt = (((1,), (1,)), ((), ()))

    key_i = lax.broadcasted_iota(jnp.int32, (TK, TQ), 0)
    qry_i = lax.broadcasted_iota(jnp.int32, (TK, TQ), 1)
    t_idx = q0 + lax.broadcasted_iota(jnp.int32, (1, TQ), 1)

    wt_ref[...] = w_ref[0].T * IDX_W_SCALE

    @pl.when(i == 0)
    def _():
        def transpose_chunk(kc, carry):
            k0 = pl.multiple_of(kc * TK, TK)
            vat_ref[kc] = va_ref[0, pl.ds(k0, TK), :].astype(F32).T.astype(BF16)
            return carry

        lax.fori_loop(0, S // TK, transpose_chunk, 0)

    def chunk_steps(step, cg):
        def group(p, carry):
            step(cg * p, cg)
            return carry

        lax.fori_loop(0, n_ch // cg, group, 0)
        size = cg // 2
        while size >= 1:
            @pl.when((n_ch & size) != 0)
            def _(size=size):
                step((n_ch // (2 * size)) * (2 * size), size)
            size //= 2

    smin_ref[...] = jnp.full(smin_ref.shape, jnp.inf, F32)
    smax_ref[...] = jnp.full(smax_ref.shape, -jnp.inf, F32)

    def idx_step(c0, nc):
        rows = nc * TK
        k0 = pl.multiple_of(c0 * TK, TK)
        kia = kia_ref[0, pl.ds(k0, rows), :]
        kib = kib_ref[0, pl.ds(k0, rows), :]
        acc = jnp.zeros((rows, TQ), F32)
        for j in range(IDX_HEADS // 2):
            qp = qi_ref[0, :, j * 128:(j + 1) * 128]
            sa = lax.dot_general(kia, qp, dn_nt, preferred_element_type=F32)
            sb = lax.dot_general(kib, qp, dn_nt, preferred_element_type=F32)
            acc = (acc + jnp.maximum(sa, 0.0) * wt_ref[2 * j:2 * j + 1, :]
                   + jnp.maximum(sb, 0.0) * wt_ref[2 * j + 1:2 * j + 2, :])
        key_r = lax.broadcasted_iota(jnp.int32, (rows, TQ), 0)
        qry_r = lax.broadcasted_iota(jnp.int32, (rows, TQ), 1)
        causal = (k0 + key_r) <= (q0 + qry_r)
        score_ref[pl.ds(c0, nc)] = jnp.where(causal, acc, -jnp.inf).reshape(nc, TK, TQ)
        smin_ref[...] = jnp.minimum(smin_ref[...],
                                    jnp.min(jnp.where(causal, acc, jnp.inf), axis=0, keepdims=True))
        smax_ref[...] = jnp.maximum(smax_ref[...],
                                    jnp.max(jnp.where(causal, acc, -jnp.inf), axis=0, keepdims=True))

    chunk_steps(idx_step, CG_IDX)
    smin, smax = smin_ref[...], smax_ref[...]

    def count_keys(pred):
        def body(kc, acc):
            m = pred(score_ref[kc], kc)
            return acc + m.reshape(TK // 32, 4, 8, TQ).sum(axis=0)

        acc = lax.fori_loop(0, n_ch, body, jnp.zeros((4, 8, TQ), F32))
        return jnp.sum(acc.sum(axis=0), axis=0, keepdims=True)

    def count_ge(t):
        return count_keys(lambda blk, kc: jnp.where(blk >= t, 1.0, 0.0))

    n_masked = (S - 1 - t_idx).astype(F32)
    n_causal = (t_idx + 1).astype(F32)

    def bis_cond(st):
        return jnp.logical_and(st[4] > 0.5, st[5] < 2 * 32 + 8)

    def bis_body(st):
        lo, hi, cnt_lo, cph_lo, _, it = st
        for step in range(STEPS):
            mid = (lo >> 1) + (hi >> 1) + (lo & hi & 1)
            if step % 2 == 0:
                f_lo, f_hi = _key_to_f32(lo), _key_to_f32(hi)
                mid_v = _f32_to_key(f_lo + (f_hi - f_lo) * 0.5)
                mid = jnp.where(jnp.logical_and(mid_v > lo, mid_v < hi), mid_v, mid)
            fm = _key_to_f32(mid)
            cph = count_ge(fm)
            c = cph + jnp.where(fm <= NEG, n_masked, 0.0)
            ge = c >= kf
            lo = jnp.where(ge, mid, lo)
            hi = jnp.where(ge, hi, mid)
            cnt_lo = jnp.where(ge, c, cnt_lo)
            cph_lo = jnp.where(ge, cph, cph_lo)
        done = jnp.logical_or(cnt_lo == kf, lo + 1 >= hi)
        return lo, hi, cnt_lo, cph_lo, jnp.max(jnp.where(done, 0.0, 1.0)), it + STEPS

    below_neg = smin < NEG
    few = jnp.logical_and(n_causal < kf, jnp.logical_not(below_neg))
    key_neg = _f32_to_key(jnp.full((1, TQ), NEG, F32))
    lo0 = jnp.where(below_neg, KEY_LO0, jnp.where(few, key_neg, _f32_to_key(smin)))
    above_max = jnp.where(jnp.abs(smax) < F32_MIN_NORMAL, F32_MIN_NORMAL_KEY, _f32_to_key(smax) + 1)
    hi0 = jnp.where(below_neg, KEY_HI0, jnp.where(few, key_neg + 1, above_max))
    cnt0 = n_causal + jnp.where(_key_to_f32(lo0) <= NEG, n_masked, 0.0)
    st0 = (lo0, hi0, cnt0, n_causal, jnp.float32(1.0), jnp.int32(0))
    lo, hi, cnt_lo, cph_lo, _, _ = lax.while_loop(bis_cond, bis_body, st0)
    thr = _key_to_f32(lo)

    f_hi = _key_to_f32(hi)
    masked_hi = jnp.where(f_hi <= NEG, n_masked, 0.0)
    need = jnp.logical_and(cnt_lo > kf, cph_lo > kf - masked_hi)
    any_need = jnp.max(jnp.where(need, 1.0, 0.0)) > 0.5

    @pl.when(any_need)
    def _fix_ties():
        slots = kf - count_ge(f_hi) - masked_hi

        def jb(_, st):
            jlo, jhi = st
            jm = (jlo + jhi) >> 1
            ties = count_keys(lambda blk, kc: jnp.where(
                blk == thr, jnp.where(kc * TK + key_i <= jm, 1.0, 0.0), 0.0))
            ge = ties >= slots
            return jnp.where(ge, jlo, jm), jnp.where(ge, jm, jhi)

        n_iter = max(1, (S - 1).bit_length())
        _, jcut = lax.fori_loop(0, n_iter, jb, (jnp.full((1, TQ), -1, jnp.int32),
                                                 jnp.full((1, TQ), S - 1, jnp.int32)))
        jcut = jnp.where(need, jcut, S)

        def drop(kc, carry):
            blk = score_ref[kc]
            surplus = jnp.where(blk == thr, jnp.where(kc * TK + key_i > jcut, 1.0, 0.0), 0.0)
            score_ref[kc] = jnp.where(surplus > 0.5, -jnp.inf, blk)
            return carry

        lax.fori_loop(0, n_ch, drop, 0)

    m_ref[...] = jnp.full(m_ref.shape, NEG, F32)
    l_ref[...] = jnp.zeros(l_ref.shape, F32)
    acc_ref[...] = jnp.zeros(acc_ref.shape, F32)

    def att_step(c0, nc):
        rows = nc * TK
        k0 = pl.multiple_of(c0 * TK, TK)
        mb_ref[:rows] = jnp.where(score_ref[pl.ds(c0, nc)].reshape(rows, TQ) >= thr, 0.0, NEG)
        kpos = kpos_ref[pl.ds(k0, rows), :]
        col_max = []
        for hh in range(N_HEADS_A):
            g = hh // HEADS_PER_KV
            kg = ka_ref[0, pl.ds(k0, rows), g * HEAD_DIM:(g + 1) * HEAD_DIM]
            qh = qa_ref[0, :, hh * HEAD_DIM:(hh + 1) * HEAD_DIM]
            s = lax.dot_general(jnp.concatenate([kg, kpos], axis=1),
                                jnp.concatenate([qh, qslope_ref[hh]], axis=1),
                                dn_nt, preferred_element_type=F32)
            lm = s * (ATTN_SCALE * LOG2E) + mb_ref[:rows]
            lm_ref[hh, :rows] = lm
            col_max.append(jnp.max(lm, axis=0, keepdims=True))
        for hh in range(N_HEADS_A):
            g = hh // HEADS_PER_KV
            vt = jnp.concatenate([vat_ref[c0 + c, g * HEAD_DIM:(g + 1) * HEAD_DIM, :] for c in range(nc)],
                                 axis=1)
            m_old = m_ref[hh]
            m_new = jnp.maximum(m_old, col_max[hh])
            alpha = jnp.exp2(m_old - m_new)
            m_ref[hh] = m_new
            p = jnp.exp2(lm_ref[hh, :rows] - m_new)
            l_ref[hh] = alpha * l_ref[hh] + jnp.sum(p, axis=0, keepdims=True)
            p_ref[hh, :rows] = p.astype(BF16)
            pv = jnp.dot(vt, p_ref[hh, :rows], preferred_element_type=F32)
            acc_ref[hh] = alpha * acc_ref[hh] + pv

    chunk_steps(att_step, CG)

    for hh in range(N_HEADS_A):
        o_t = acc_ref[hh] / l_ref[hh]
        o_ref[0, :, hh * HEAD_DIM:(hh + 1) * HEAD_DIM] = o_t.T.astype(BF16)


POS_RADIX = 64
N_SPLIT = 3


def _alibi_operands(n_pos, TQ, slopes):
    assert n_pos <= POS_RADIX * POS_RADIX
    s = np.arange(n_pos)
    kpos = np.zeros((n_pos, 128), np.float32)
    kpos[:, :N_SPLIT] = (s // POS_RADIX)[:, None]
    kpos[:, N_SPLIT:2 * N_SPLIT] = (s % POS_RADIX)[:, None]
    qslope = np.zeros((len(slopes), 128), np.float32)
    for h, slope in enumerate(slopes):
        rest = np.float32(slope / ATTN_SCALE)
        for n in range(N_SPLIT):
            piece = np.float32(rest.astype(BF16))
            qslope[h, n], qslope[h, N_SPLIT + n] = piece * POS_RADIX, piece
            rest = np.float32(rest - piece)
    qslope = np.broadcast_to(qslope[:, None, :], (len(slopes), TQ, 128))
    return jnp.asarray(kpos.astype(BF16)), jnp.asarray(qslope.astype(BF16))


def _dsa_attention(proj, tail):
    B, S, _ = proj.shape
    TQ = TK = 256
    n_kc = S // TK
    kpos, qslope = _alibi_operands(S, TQ, SLOPES_A)
    CG = 2
    CG_IDX = 4
    kern = functools.partial(_dsa_kernel, TQ=TQ, TK=TK, S=S, STEPS=2, CG=CG, CG_IDX=CG_IDX)
    return pl.pallas_call(
        kern,
        grid=(B, S // TQ),
        in_specs=[pl.BlockSpec((1, TQ, 1024), lambda b, i: (b, i, COL_QA // 1024)),
                  pl.BlockSpec((1, TQ, 1024), lambda b, i: (b, i, COL_QI // 1024)),
                  pl.BlockSpec((1, TQ, 128), lambda b, i: (b, i, 0)),
                  pl.BlockSpec((1, S, 256), lambda b, i: (b, 0, COL_KA // 256)),
                  pl.BlockSpec((1, S, 256), lambda b, i: (b, 0, COL_VA // 256)),
                  pl.BlockSpec((1, S, 128), lambda b, i: (b, 0, COL_KIA // 128)),
                  pl.BlockSpec((1, S, 128), lambda b, i: (b, 0, COL_KIB // 128)),
                  pl.BlockSpec((S, 128), lambda b, i: (0, 0)),
                  pl.BlockSpec((N_HEADS_A, TQ, 128), lambda b, i: (0, 0, 0))],
        out_specs=pl.BlockSpec((1, TQ, 1024), lambda b, i: (b, i, 0)),
        out_shape=jax.ShapeDtypeStruct((B, S, N_HEADS_A * HEAD_DIM), BF16),
        scratch_shapes=[pltpu.VMEM((n_kc, N_KV * HEAD_DIM, TK), BF16),
                        pltpu.VMEM((n_kc, TK, TQ), F32),
                        pltpu.VMEM((128, TQ), F32),
                        pltpu.VMEM((1, TQ), F32),
                        pltpu.VMEM((1, TQ), F32),
                        pltpu.VMEM((N_HEADS_A, CG * TK, TQ), F32),
                        pltpu.VMEM((N_HEADS_A, CG * TK, TQ), BF16),
                        pltpu.VMEM((CG * TK, TQ), F32),
                        pltpu.VMEM((N_HEADS_A, 1, TQ), F32),
                        pltpu.VMEM((N_HEADS_A, 1, TQ), F32),
                        pltpu.VMEM((N_HEADS_A, HEAD_DIM, TQ), F32)],
        compiler_params=_cparams(("arbitrary", "arbitrary")),
        name="dsa_attn",
    )(proj, proj, tail, proj, proj, proj, proj, kpos, qslope)


def _swa_kernel(sink_ref, q_ref, k_ref, v_ref, kpos_ref, qslope_ref, o_ref, lm_ref, p_ref, mb_ref, *, TQ, KW):
    i = pl.program_id(1)
    q0 = i * TQ
    start = pl.multiple_of(jnp.maximum(q0 - WINDOW, 0), WINDOW)
    dn_nt = (((1,), (1,)), ((), ()))
    key_r = lax.broadcasted_iota(jnp.int32, (KW, TQ), 0)
    qry_r = lax.broadcasted_iota(jnp.int32, (KW, TQ), 1)
    dist = (q0 + qry_r) - (start + key_r)
    mb_ref[...] = jnp.where(dist >= 0, jnp.where(dist < WINDOW, 0.0, NEG), NEG)
    t_rel = (q0 - start + lax.broadcasted_iota(jnp.int32, (1, TQ), 1)).astype(F32)
    kpos = kpos_ref[...]
    for hh in range(N_HEADS_B):
        g = hh // HEADS_PER_KV
        kg = k_ref[0, pl.ds(start, KW), g * HEAD_DIM:(g + 1) * HEAD_DIM]
        qh = q_ref[0, :, hh * HEAD_DIM:(hh + 1) * HEAD_DIM]
        lm_ref[hh] = lax.dot_general(jnp.concatenate([kg, kpos], axis=1),
                                     jnp.concatenate([qh, qslope_ref[hh]], axis=1),
                                     dn_nt, preferred_element_type=F32)
    vts = [v_ref[0, pl.ds(start, KW), g * HEAD_DIM:(g + 1) * HEAD_DIM].astype(F32).T.astype(BF16)
           for g in range(N_KV)]
    for hh in range(N_HEADS_B):
        lm = lm_ref[hh] * (ATTN_SCALE * LOG2E) + mb_ref[...]
        sink = (sink_ref[hh] + SLOPES_B[hh] * t_rel) * LOG2E
        m = jnp.maximum(jnp.max(lm, axis=0, keepdims=True), sink)
        p = jnp.exp2(lm - m)
        den = jnp.sum(p, axis=0, keepdims=True) + jnp.exp2(sink - m)
        p_ref[hh] = p.astype(BF16)
        pv = jnp.dot(vts[hh // HEADS_PER_KV], p_ref[hh], preferred_element_type=F32)
        o_ref[0, :, hh * HEAD_DIM:(hh + 1) * HEAD_DIM] = (pv / den).T.astype(BF16)


def _swa_attention(proj, sinks):
    B, S, _ = proj.shape
    TQ = 256
    KW = TQ + WINDOW
    kpos, qslope = _alibi_operands(KW, TQ, SLOPES_B)
    kern = functools.partial(_swa_kernel, TQ=TQ, KW=KW)
    return pl.pallas_call(
        kern,
        grid=(B, S // TQ),
        in_specs=[pl.BlockSpec(memory_space=pltpu.SMEM),
                  pl.BlockSpec((1, TQ, 1024), lambda b, i: (b, i, COL_QB // 1024)),
                  pl.BlockSpec((1, S, 256), lambda b, i: (b, 0, COL_KB // 256)),
                  pl.BlockSpec((1, S, 256), lambda b, i: (b, 0, COL_VB // 256)),
                  pl.BlockSpec((KW, 128), lambda b, i: (0, 0)),
                  pl.BlockSpec((N_HEADS_B, TQ, 128), lambda b, i: (0, 0, 0))],
        out_specs=pl.BlockSpec((1, TQ, 1024), lambda b, i: (b, i, 0)),
        out_shape=jax.ShapeDtypeStruct((B, S, N_HEADS_B * HEAD_DIM), BF16),
        scratch_shapes=[pltpu.VMEM((N_HEADS_B, KW, TQ), F32),
                        pltpu.VMEM((N_HEADS_B, KW, TQ), BF16),
                        pltpu.VMEM((KW, TQ), F32)],
        compiler_params=_cparams(("arbitrary", "arbitrary")),
        name="swa_attn",
    )(sinks, proj, proj, proj, kpos, qslope)


def _outproj_kernel(oa_ref, ob_ref, x_ref, mod_ref, g_ref, w_ref, x1_ref, h2_ref):
    half = oa_ref.shape[2]
    mix = jnp.dot(oa_ref[0], w_ref[:half, :], preferred_element_type=F32)
    mix = mix + jnp.dot(ob_ref[0], w_ref[half:, :], preferred_element_type=F32)
    x1 = x_ref[0] + mod_ref[0, 2:3, :] * mix
    x1_ref[0] = x1
    ms = jnp.mean(x1 * x1, axis=-1, keepdims=True)
    y = x1 * lax.rsqrt(ms + EPS) * g_ref[...]
    h2_ref[0] = (y * (1.0 + mod_ref[0, 4:5, :]) + mod_ref[0, 3:4, :]).astype(BF16)


def _out_proj(oa, ob, x, mod, g_ffn, w_o):
    B, S, D = x.shape
    tm = 512
    half = oa.shape[2]
    return pl.pallas_call(
        _outproj_kernel,
        grid=(B, S // tm),
        in_specs=[pl.BlockSpec((1, tm, half), lambda b, i: (b, i, 0)),
                  pl.BlockSpec((1, tm, half), lambda b, i: (b, i, 0)),
                  pl.BlockSpec((1, tm, D), lambda b, i: (b, i, 0)),
                  pl.BlockSpec((1, N_MOD, D), lambda b, i: (b, 0, 0)),
                  pl.BlockSpec((1, D), lambda b, i: (0, 0)),
                  pl.BlockSpec((2 * half, D), lambda b, i: (0, 0))],
        out_specs=[pl.BlockSpec((1, tm, D), lambda b, i: (b, i, 0)),
                   pl.BlockSpec((1, tm, D), lambda b, i: (b, i, 0))],
        out_shape=[jax.ShapeDtypeStruct((B, S, D), F32),
                   jax.ShapeDtypeStruct((B, S, D), BF16)],
        compiler_params=_cparams(("arbitrary", "arbitrary")),
        name="out_proj",
    )(oa, ob, x, mod, g_ffn.reshape(1, D), w_o)


def _ffn_kernel(h2_ref, x1_ref, mod_ref, wg_ref, wu_ref, cw_ref, cb_ref, wd_ref, gf_ref, o_ref,
                acc_ref, carry_ref, *, tm):
    i = pl.program_id(1)
    j = pl.program_id(2)
    nj = pl.num_programs(2)

    @pl.when(j == 0)
    def _():
        acc_ref[...] = jnp.zeros(acc_ref.shape, F32)

    h2 = h2_ref[0]
    g = jnp.dot(h2, wg_ref[...], preferred_element_type=F32)
    u = jnp.dot(h2, wu_ref[...], preferred_element_type=F32)

    prev = jnp.where(i > 0, carry_ref[j], 0.0)
    carry_ref[j] = g[tm - 8:, :]
    row = lax.broadcasted_iota(jnp.int32, g.shape, 0)
    g1 = jnp.where(row == 0, prev[7:8, :], pltpu.roll(g, 1, axis=0))
    g2 = jnp.where(row == 0, prev[6:7, :], jnp.where(row == 1, prev[7:8, :], pltpu.roll(g, 2, axis=0)))
    gc = cw_ref[0:1, :] * g2 + cw_ref[1:2, :] * g1 + cw_ref[2:3, :] * g + cb_ref[...]
    a = gc / (1.0 + jnp.exp(-gc)) * u
    acc_ref[...] += jnp.dot(a.astype(BF16), wd_ref[...], preferred_element_type=F32)

    @pl.when(j == nj - 1)
    def _():
        x2 = x1_ref[0] + mod_ref[0, 5:6, :] * acc_ref[...]
        ms = jnp.mean(x2 * x2, axis=-1, keepdims=True)
        o_ref[0] = x2 * lax.rsqrt(ms + EPS) * gf_ref[...]


def _conv_ffn(h2, x1, mod, w_gate, w_up, conv_w, conv_b, w_down, g_final):
    B, S, D = x1.shape
    F = w_gate.shape[1]
    tm, tf = 512, 512
    kern = functools.partial(_ffn_kernel, tm=tm)
    return pl.pallas_call(
        kern,
        grid=(B, S // tm, F // tf),
        in_specs=[pl.BlockSpec((1, tm, D), lambda b, i, j: (b, i, 0)),
                  pl.BlockSpec((1, tm, D), lambda b, i, j: (b, i, 0)),
                  pl.BlockSpec((1, N_MOD, D), lambda b, i, j: (b, 0, 0)),
                  pl.BlockSpec((D, tf), lambda b, i, j: (0, j)),
                  pl.BlockSpec((D, tf), lambda b, i, j: (0, j)),
                  pl.BlockSpec((3, tf), lambda b, i, j: (0, j)),
                  pl.BlockSpec((1, tf), lambda b, i, j: (0, j)),
                  pl.BlockSpec((tf, D), lambda b, i, j: (j, 0)),
                  pl.BlockSpec((1, D), lambda b, i, j: (0, 0))],
        out_specs=pl.BlockSpec((1, tm, D), lambda b, i, j: (b, i, 0)),
        out_shape=jax.ShapeDtypeStruct((B, S, D), F32),
        scratch_shapes=[pltpu.VMEM((tm, D), F32),
                        pltpu.VMEM((F // tf, 8, tf), F32)],
        compiler_params=_cparams(("arbitrary", "arbitrary", "arbitrary")),
        name="conv_ffn",
    )(h2, x1, mod, w_gate, w_up, conv_w, conv_b.reshape(1, F), w_down, g_final.reshape(1, D))


def _regroup_w_in(w_in):
    sizes = [1024, 256, 256, 1024, 64, 16, 1024, 256, 256]
    offs = [0]
    for s in sizes:
        offs.append(offs[-1] + s)
    w_t = w_in.T
    qa, ka, va, qi, ki, wi, qb, kb, vb = [w_t[offs[n]:offs[n + 1]] for n in range(9)]
    z64 = jnp.zeros_like(ki)
    big = jnp.concatenate([qa, qi, qb, ka, va, kb, vb, ki, z64, z64, ki], axis=0).astype(BF16)
    tail = jnp.concatenate([wi, jnp.zeros((128 - IDX_HEADS, w_in.shape[0]), w_in.dtype)], axis=0).astype(BF16)
    return big, tail


def kernel(x, c, w_ada, b_ada, g_mix, w_in, sinks, w_o, g_ffn, w_gate, w_up, conv_w, conv_b, w_down, g_final):
    assert w_ada.shape[0] == 1, "the final norm is fused into the (single) layer's FFN kernel"
    mod = _ada_mod(c, w_ada[0], b_ada[0])
    w_big, w_tail = _regroup_w_in(w_in[0])
    proj, tail, (wo_b, wg_b, wu_b, wd_b) = _in_proj(x, mod, g_mix[0], w_big, w_tail,
                                                    (w_o[0], w_gate[0], w_up[0], w_down[0]))
    oa = _dsa_attention(proj, tail)
    ob = _swa_attention(proj, sinks[0])
    x1, h2 = _out_proj(oa, ob, x, mod, g_ffn[0], wo_b)
    return _conv_ffn(h2, x1, mod, wg_b, wu_b, conv_w[0], conv_b[0], wd_b, g_final)
```

```python
import functools

import jax
import jax.numpy as jnp
import numpy as np
from jax import lax
from jax.experimental import pallas as pl
from jax.experimental.pallas import tpu as pltpu

F32 = jnp.float32
BF16 = jnp.bfloat16

HEAD_DIM = 128
N_HEADS_A = 8
N_HEADS_B = 8
N_KV = 2
HEADS_PER_KV = 4
IDX_HEADS = 16
IDX_DIM = 64
TOPK_MAX = 256
WINDOW = 128
N_MOD = 6
EPS = 1e-6
NEG = -1e30
ATTN_SCALE = HEAD_DIM ** -0.5
LOG2E = 1.4426950408889634
IDX_W_SCALE = (IDX_DIM ** -0.5) * (IDX_HEADS ** -0.5)
SLOPES = [2.0 ** (-8.0 * j / 16.0) for j in range(1, 17)]
SLOPES_A = SLOPES[0::2]
SLOPES_B = SLOPES[1::2]

COL_QA, COL_QI, COL_QB = 0, 1024, 2048
COL_KA, COL_VA, COL_KB, COL_VB = 3072, 3328, 3584, 3840
COL_KIA, COL_KIB = 4096, 4224
PROJ_W = 4352

VMEM_LIMIT = 56 * 1024 * 1024

KEY_LO0 = -2139095040
KEY_HI0 = 2139095040
F32_MIN_NORMAL = 1.1754943508222875e-38
F32_MIN_NORMAL_KEY = 0x00800000


def _cparams(sem):
    return pltpu.CompilerParams(dimension_semantics=sem, vmem_limit_bytes=VMEM_LIMIT)


def _ada_kernel(c_ref, w_ref, b_ref, o_ref):
    c = c_ref[...]
    s = c / (1.0 + jnp.exp(-c))
    o_ref[...] = jnp.dot(s.astype(BF16), w_ref[...].astype(BF16),
                         preferred_element_type=F32) + b_ref[...]


def _ada_mod(c, w_ada, b_ada):
    B, D = c.shape
    N = w_ada.shape[1]
    tn = 1024
    cp = jnp.zeros((8, D), F32).at[:B].set(c)
    out = pl.pallas_call(
        _ada_kernel,
        grid=(N // tn,),
        in_specs=[pl.BlockSpec((8, D), lambda j: (0, 0)),
                  pl.BlockSpec((D, tn), lambda j: (0, j)),
                  pl.BlockSpec((1, tn), lambda j: (0, j))],
        out_specs=pl.BlockSpec((8, tn), lambda j: (0, j)),
        out_shape=jax.ShapeDtypeStruct((8, N), F32),
        compiler_params=_cparams(("arbitrary",)),
        name="ada_mod",
    )(cp, w_ada, b_ada.reshape(1, N))
    return out[:B].reshape(B, N_MOD, D)


def _inproj_kernel(x_ref, mod_ref, g_ref, w_ref, wt_ref, *rest, chunk, n_cast):
    cast_in, (p_ref, t_ref), cast_out = rest[:n_cast], rest[n_cast:n_cast + 2], rest[n_cast + 2:]
    for src, dst in zip(cast_in, cast_out):
        dst[...] = src[...].astype(BF16)
    x = x_ref[0]
    ms = jnp.mean(x * x, axis=-1, keepdims=True)
    y = x * lax.rsqrt(ms + EPS) * g_ref[...]
    h = y * (1.0 + mod_ref[0, 1:2, :]) + mod_ref[0, 0:1, :]
    hb = h.astype(BF16)
    dn_nt = (((1,), (1,)), ((), ()))
    for n in range(0, PROJ_W, chunk):
        p_ref[0, :, n:n + chunk] = lax.dot_general(
            hb, w_ref[n:n + chunk, :], dn_nt, preferred_element_type=F32).astype(BF16)
    t_ref[0] = lax.dot_general(hb, wt_ref[...], dn_nt, preferred_element_type=F32)


def _in_proj(x, mod, g_mix, w_big, w_tail, f32_weights):
    B, S, D = x.shape
    tm = 512
    n_steps = B * (S // tm)
    cast_specs = []
    for w in f32_weights:
        rows = w.shape[0] // n_steps
        assert rows * n_steps == w.shape[0] and rows % 16 == 0, w.shape
        cast_specs.append(pl.BlockSpec((rows, w.shape[1]), lambda b, i: (b * (S // tm) + i, 0)))
    outs = pl.pallas_call(
        functools.partial(_inproj_kernel, chunk=256, n_cast=len(f32_weights)),
        grid=(B, S // tm),
        in_specs=[pl.BlockSpec((1, tm, D), lambda b, i: (b, i, 0)),
                  pl.BlockSpec((1, N_MOD, D), lambda b, i: (b, 0, 0)),
                  pl.BlockSpec((1, D), lambda b, i: (0, 0)),
                  pl.BlockSpec((PROJ_W, D), lambda b, i: (0, 0)),
                  pl.BlockSpec((128, D), lambda b, i: (0, 0))] + cast_specs,
        out_specs=[pl.BlockSpec((1, tm, PROJ_W), lambda b, i: (b, i, 0)),
                   pl.BlockSpec((1, tm, 128), lambda b, i: (b, i, 0))] + cast_specs,
        out_shape=[jax.ShapeDtypeStruct((B, S, PROJ_W), BF16),
                   jax.ShapeDtypeStruct((B, S, 128), F32)]
                  + [jax.ShapeDtypeStruct(w.shape, BF16) for w in f32_weights],
        compiler_params=_cparams(("arbitrary", "arbitrary")),
        name="in_proj",
    )(x, mod, g_mix.reshape(1, D), w_big, w_tail, *f32_weights)
    return outs[0], outs[1], outs[2:]


def _key_to_f32(k):
    bits = k ^ ((k >> 31) & 0x7FFFFFFF)
    return lax.bitcast_convert_type(bits, F32)


def _f32_to_key(f):
    bits = lax.bitcast_convert_type(f, jnp.int32)
    return bits ^ ((bits >> 31) & 0x7FFFFFFF)


def _dsa_kernel(qa_ref, qi_ref, w_ref, ka_ref, va_ref, kia_ref, kib_ref, kpos_ref, qslope_ref, o_ref,
                vat_ref, score_ref, wt_ref, smin_ref, smax_ref, lm_ref, p_ref, mb_ref, m_ref, l_ref, acc_ref,
                *, TQ, TK, S, STEPS, CG, CG_IDX):
    i = pl.program_id(1)
    q0 = i * TQ
    n_ch = i + 1
    kf = float(TOPK_MAX)
    dn_nt = (((1,), (1,)), ((), ()))

    key_i = lax.broadcasted_iota(jnp.int32, (TK, TQ), 0)
    qry_i = lax.broadcasted_iota(jnp.int32, (TK, TQ), 1)
    t_idx = q0 + lax.broadcasted_iota(jnp.int32, (1, TQ), 1)

    wt_ref[...] = w_ref[0].T * IDX_W_SCALE

    @pl.when(i == 0)
    def _():
        def transpose_chunk(kc, carry):
            k0 = pl.multiple_of(kc * TK, TK)
            vat_ref[kc] = va_ref[0, pl.ds(k0, TK), :].astype(F32).T.astype(BF16)
            return carry

        lax.fori_loop(0, S // TK, transpose_chunk, 0)

    def chunk_steps(step, cg):
        def group(p, carry):
            step(cg * p, cg)
            return carry

        lax.fori_loop(0, n_ch // cg, group, 0)
        size = cg // 2
        while size >= 1:
            @pl.when((n_ch & size) != 0)
            def _(size=size):
                step((n_ch // (2 * size)) * (2 * size), size)
            size //= 2

    smin_ref[...] = jnp.full(smin_ref.shape, jnp.inf, F32)
    smax_ref[...] = jnp.full(smax_ref.shape, -jnp.inf, F32)

    def idx_step(c0, nc):
        rows = nc * TK
        k0 = pl.multiple_of(c0 * TK, TK)
        kia = kia_ref[0, pl.ds(k0, rows), :]
        kib = kib_ref[0, pl.ds(k0, rows), :]
        acc = jnp.zeros((rows, TQ), F32)
        for j in range(IDX_HEADS // 2):
            qp = qi_ref[0, :, j * 128:(j + 1) * 128]
            sa = lax.dot_general(kia, qp, dn_nt, preferred_element_type=F32)
            sb = lax.dot_general(kib, qp, dn_nt, preferred_element_type=F32)
            acc = (acc + jnp.maximum(sa, 0.0) * wt_ref[2 * j:2 * j + 1, :]
                   + jnp.maximum(sb, 0.0) * wt_ref[2 * j + 1:2 * j + 2, :])
        key_r = lax.broadcasted_iota(jnp.int32, (rows, TQ), 0)
        qry_r = lax.broadcasted_iota(jnp.int32, (rows, TQ), 1)
        causal = (k0 + key_r) <= (q0 + qry_r)
        score_ref[pl.ds(c0, nc)] = jnp.where(causal, acc, -jnp.inf).reshape(nc, TK, TQ)
        smin_ref[...] = jnp.minimum(smin_ref[...],
                                    jnp.min(jnp.where(causal, acc, jnp.inf), axis=0, keepdims=True))
        smax_ref[...] = jnp.maximum(smax_ref[...],
                                    jnp.max(jnp.where(causal, acc, -jnp.inf), axis=0, keepdims=True))

    chunk_steps(idx_step, CG_IDX)
    smin, smax = smin_ref[...], smax_ref[...]

    def count_keys(pred):
        def body(kc, acc):
            m = pred(score_ref[kc], kc)
            return acc + m.reshape(TK // 32, 4, 8, TQ).sum(axis=0)

        acc = lax.fori_loop(0, n_ch, body, jnp.zeros((4, 8, TQ), F32))
        return jnp.sum(acc.sum(axis=0), axis=0, keepdims=True)

    def count_ge(t):
        return count_keys(lambda blk, kc: jnp.where(blk >= t, 1.0, 0.0))

    n_masked = (S - 1 - t_idx).astype(F32)
    n_causal = (t_idx + 1).astype(F32)

    def bis_cond(st):
        return jnp.logical_and(st[4] > 0.5, st[5] < 2 * 32 + 8)

    def bis_body(st):
        lo, hi, cnt_lo, cph_lo, _, it = st
        for step in range(STEPS):
            mid = (lo >> 1) + (hi >> 1) + (lo & hi & 1)
            if step % 2 == 0:
                f_lo, f_hi = _key_to_f32(lo), _key_to_f32(hi)
                mid_v = _f32_to_key(f_lo + (f_hi - f_lo) * 0.5)
                mid = jnp.where(jnp.logical_and(mid_v > lo, mid_v < hi), mid_v, mid)
            fm = _key_to_f32(mid)
            cph = count_ge(fm)
            c = cph + jnp.where(fm <= NEG, n_masked, 0.0)
            ge = c >= kf
            lo = jnp.where(ge, mid, lo)
            hi = jnp.where(ge, hi, mid)
            cnt_lo = jnp.where(ge, c, cnt_lo)
            cph_lo = jnp.where(ge, cph, cph_lo)
        done = jnp.logical_or(cnt_lo == kf, lo + 1 >= hi)
        return lo, hi, cnt_lo, cph_lo, jnp.max(jnp.where(done, 0.0, 1.0)), it + STEPS

    below_neg = smin < NEG
    few = jnp.logical_and(n_causal < kf, jnp.logical_not(below_neg))
    key_neg = _f32_to_key(jnp.full((1, TQ), NEG, F32))
    lo0 = jnp.where(below_neg, KEY_LO0, jnp.where(few, key_neg, _f32_to_key(smin)))
    above_max = jnp.where(jnp.abs(smax) < F32_MIN_NORMAL, F32_MIN_NORMAL_KEY, _f32_to_key(smax) + 1)
    hi0 = jnp.where(below_neg, KEY_HI0, jnp.where(few, key_neg + 1, above_max))
    cnt0 = n_causal + jnp.where(_key_to_f32(lo0) <= NEG, n_masked, 0.0)
    st0 = (lo0, hi0, cnt0, n_causal, jnp.float32(1.0), jnp.int32(0))
    lo, hi, cnt_lo, cph_lo, _, _ = lax.while_loop(bis_cond, bis_body, st0)
    thr = _key_to_f32(lo)

    f_hi = _key_to_f32(hi)
    masked_hi = jnp.where(f_hi <= NEG, n_masked, 0.0)
    need = jnp.logical_and(cnt_lo > kf, cph_lo > kf - masked_hi)
    any_need = jnp.max(jnp.where(need, 1.0, 0.0)) > 0.5

    @pl.when(any_need)
    def _fix_ties():
        slots = kf - count_ge(f_hi) - masked_hi

        def jb(_, st):
            jlo, jhi = st
            jm = (jlo + jhi) >> 1
            ties = count_keys(lambda blk, kc: jnp.where(
                blk == thr, jnp.where(kc * TK + key_i <= jm, 1.0, 0.0), 0.0))
            ge = ties >= slots
            return jnp.where(ge, jlo, jm), jnp.where(ge, jm, jhi)

        n_iter = max(1, (S - 1).bit_length())
        _, jcut = lax.fori_loop(0, n_iter, jb, (jnp.full((1, TQ), -1, jnp.int32),
                                                 jnp.full((1, TQ), S - 1, jnp.int32)))
        jcut = jnp.where(need, jcut, S)

        def drop(kc, carry):
            blk = score_ref[kc]
            surplus = jnp.where(blk == thr, jnp.where(kc * TK + key_i > jcut, 1.0, 0.0), 0.0)
            score_ref[kc] = jnp.where(surplus > 0.5, -jnp.inf, blk)
            return carry

        lax.fori_loop(0, n_ch, drop, 0)

    m_ref[...] = jnp.full(m_ref.shape, NEG, F32)
    l_ref[...] = jnp.zeros(l_ref.shape, F32)
    acc_ref[...] = jnp.zeros(acc_ref.shape, F32)

    def att_step(c0, nc):
        rows = nc * TK
        k0 = pl.multiple_of(c0 * TK, TK)
        mb_ref[:rows] = jnp.where(score_ref[pl.ds(c0, nc)].reshape(rows, TQ) >= thr, 0.0, NEG)
        kpos = kpos_ref[pl.ds(k0, rows), :]
        col_max = []
        for hh in range(N_HEADS_A):
            g = hh // HEADS_PER_KV
            kg = ka_ref[0, pl.ds(k0, rows), g * HEAD_DIM:(g + 1) * HEAD_DIM]
            qh = qa_ref[0, :, hh * HEAD_DIM:(hh + 1) * HEAD_DIM]
            s = lax.dot_general(jnp.concatenate([kg, kpos], axis=1),
                                jnp.concatenate([qh, qslope_ref[hh]], axis=1),
                                dn_nt, preferred_element_type=F32)
            lm = s * (ATTN_SCALE * LOG2E) + mb_ref[:rows]
            lm_ref[hh, :rows] = lm
            col_max.append(jnp.max(lm, axis=0, keepdims=True))
        for hh in range(N_HEADS_A):
            g = hh // HEADS_PER_KV
            vt = jnp.concatenate([vat_ref[c0 + c, g * HEAD_DIM:(g + 1) * HEAD_DIM, :] for c in range(nc)],
                                 axis=1)
            m_old = m_ref[hh]
            m_new = jnp.maximum(m_old, col_max[hh])
            alpha = jnp.exp2(m_old - m_new)
            m_ref[hh] = m_new
            p = jnp.exp2(lm_ref[hh, :rows] - m_new)
            l_ref[hh] = alpha * l_ref[hh] + jnp.sum(p, axis=0, keepdims=True)
            p_ref[hh, :rows] = p.astype(BF16)
            pv = jnp.dot(vt, p_ref[hh, :rows], preferred_element_type=F32)
            acc_ref[hh] = alpha * acc_ref[hh] + pv

    chunk_steps(att_step, CG)

    for hh in range(N_HEADS_A):
        o_t = acc_ref[hh] / l_ref[hh]
        o_ref[0, :, hh * HEAD_DIM:(hh + 1) * HEAD_DIM] = o_t.T.astype(BF16)


POS_RADIX = 64
N_SPLIT = 3


def _alibi_operands(n_pos, TQ, slopes):
    assert n_pos <= POS_RADIX * POS_RADIX
    s = np.arange(n_pos)
    kpos = np.zeros((n_pos, 128), np.float32)
    kpos[:, :N_SPLIT] = (s // POS_RADIX)[:, None]
    kpos[:, N_SPLIT:2 * N_SPLIT] = (s % POS_RADIX)[:, None]
    qslope = np.zeros((len(slopes), 128), np.float32)
    for h, slope in enumerate(slopes):
        rest = np.float32(slope / ATTN_SCALE)
        for n in range(N_SPLIT):
            piece = np.float32(rest.astype(BF16))
            qslope[h, n], qslope[h, N_SPLIT + n] = piece * POS_RADIX, piece
            rest = np.float32(rest - piece)
    qslope = np.broadcast_to(qslope[:, None, :], (len(slopes), TQ, 128))
    return jnp.asarray(kpos.astype(BF16)), jnp.asarray(qslope.astype(BF16))


def _dsa_attention(proj, tail):
    B, S, _ = proj.shape
    TQ = TK = 256
    n_kc = S // TK
    kpos, qslope = _alibi_operands(S, TQ, SLOPES_A)
    CG = 2
    CG_IDX = 4
    kern = functools.partial(_dsa_kernel, TQ=TQ, TK=TK, S=S, STEPS=2, CG=CG, CG_IDX=CG_IDX)
    return pl.pallas_call(
        kern,
        grid=(B, S // TQ),
        in_specs=[pl.BlockSpec((1, TQ, 1024), lambda b, i: (b, i, COL_QA // 1024)),
                  pl.BlockSpec((1, TQ, 1024), lambda b, i: (b, i, COL_QI // 1024)),
                  pl.BlockSpec((1, TQ, 128), lambda b, i: (b, i, 0)),
                  pl.BlockSpec((1, S, 256), lambda b, i: (b, 0, COL_KA // 256)),
                  pl.BlockSpec((1, S, 256), lambda b, i: (b, 0, COL_VA // 256)),
                  pl.BlockSpec((1, S, 128), lambda b, i: (b, 0, COL_KIA // 128)),
                  pl.BlockSpec((1, S, 128), lambda b, i: (b, 0, COL_KIB // 128)),
                  pl.BlockSpec((S, 128), lambda b, i: (0, 0)),
                  pl.BlockSpec((N_HEADS_A, TQ, 128), lambda b, i: (0, 0, 0))],
        out_specs=pl.BlockSpec((1, TQ, 1024), lambda b, i: (b, i, 0)),
        out_shape=jax.ShapeDtypeStruct((B, S, N_HEADS_A * HEAD_DIM), BF16),
        scratch_shapes=[pltpu.VMEM((n_kc, N_KV * HEAD_DIM, TK), BF16),
                        pltpu.VMEM((n_kc, TK, TQ), F32),
                        pltpu.VMEM((128, TQ), F32),
                        pltpu.VMEM((1, TQ), F32),
                        pltpu.VMEM((1, TQ), F32),
                        pltpu.VMEM((N_HEADS_A, CG * TK, TQ), F32),
                        pltpu.VMEM((N_HEADS_A, CG * TK, TQ), BF16),
                        pltpu.VMEM((CG * TK, TQ), F32),
                        pltpu.VMEM((N_HEADS_A, 1, TQ), F32),
                        pltpu.VMEM((N_HEADS_A, 1, TQ), F32),
                        pltpu.VMEM((N_HEADS_A, HEAD_DIM, TQ), F32)],
        compiler_params=_cparams(("arbitrary", "arbitrary")),
        name="dsa_attn",
    )(proj, proj, tail, proj, proj, proj, proj, kpos, qslope)


def _swa_kernel(sink_ref, q_ref, k_ref, v_ref, kpos_ref, qslope_ref, o_ref, lm_ref, p_ref, mb_ref, *, TQ, KW):
    i = pl.program_id(1)
    q0 = i * TQ
    start = pl.multiple_of(jnp.maximum(q0 - WINDOW, 0), WINDOW)
    dn_nt = (((1,), (1,)), ((), ()))
    key_r = lax.broadcasted_iota(jnp.int32, (KW, TQ), 0)
    qry_r = lax.broadcasted_iota(jnp.int32, (KW, TQ), 1)
    dist = (q0 + qry_r) - (start + key_r)
    mb_ref[...] = jnp.where(dist >= 0, jnp.where(dist < WINDOW, 0.0, NEG), NEG)
    t_rel = (q0 - start + lax.broadcasted_iota(jnp.int32, (1, TQ), 1)).astype(F32)
    kpos = kpos_ref[...]
    for hh in range(N_HEADS_B):
        g = hh // HEADS_PER_KV
        kg = k_ref[0, pl.ds(start, KW), g * HEAD_DIM:(g + 1) * HEAD_DIM]
        qh = q_ref[0, :, hh * HEAD_DIM:(hh + 1) * HEAD_DIM]
        lm_ref[hh] = lax.dot_general(jnp.concatenate([kg, kpos], axis=1),
                                     jnp.concatenate([qh, qslope_ref[hh]], axis=1),
                                     dn_nt, preferred_element_type=F32)
    vts = [v_ref[0, pl.ds(start, KW), g * HEAD_DIM:(g + 1) * HEAD_DIM].astype(F32).T.astype(BF16)
           for g in range(N_KV)]
    for hh in range(N_HEADS_B):
        lm = lm_ref[hh] * (ATTN_SCALE * LOG2E) + mb_ref[...]
        sink = (sink_ref[hh] + SLOPES_B[hh] * t_rel) * LOG2E
        m = jnp.maximum(jnp.max(lm, axis=0, keepdims=True), sink)
        p = jnp.exp2(lm - m)
        den = jnp.sum(p, axis=0, keepdims=True) + jnp.exp2(sink - m)
        p_ref[hh] = p.astype(BF16)
        pv = jnp.dot(vts[hh // HEADS_PER_KV], p_ref[hh], preferred_element_type=F32)
        o_ref[0, :, hh * HEAD_DIM:(hh + 1) * HEAD_DIM] = (pv / den).T.astype(BF16)


def _swa_attention(proj, sinks):
    B, S, _ = proj.shape
    TQ = 256
    KW = TQ + WINDOW
    kpos, qslope = _alibi_operands(KW, TQ, SLOPES_B)
    kern = functools.partial(_swa_kernel, TQ=TQ, KW=KW)
    return pl.pallas_call(
        kern,
        grid=(B, S // TQ),
        in_specs=[pl.BlockSpec(memory_space=pltpu.SMEM),
                  pl.BlockSpec((1, TQ, 1024), lambda b, i: (b, i, COL_QB // 1024)),
                  pl.BlockSpec((1, S, 256), lambda b, i: (b, 0, COL_KB // 256)),
                  pl.BlockSpec((1, S, 256), lambda b, i: (b, 0, COL_VB // 256)),
                  pl.BlockSpec((KW, 128), lambda b, i: (0, 0)),
                  pl.BlockSpec((N_HEADS_B, TQ, 128), lambda b, i: (0, 0, 0))],
        out_specs=pl.BlockSpec((1, TQ, 1024), lambda b, i: (b, i, 0)),
        out_shape=jax.ShapeDtypeStruct((B, S, N_HEADS_B * HEAD_DIM), BF16),
        scratch_shapes=[pltpu.VMEM((N_HEADS_B, KW, TQ), F32),
                        pltpu.VMEM((N_HEADS_B, KW, TQ), BF16),
                        pltpu.VMEM((KW, TQ), F32)],
        compiler_params=_cparams(("arbitrary", "arbitrary")),
        name="swa_attn",
    )(sinks, proj, proj, proj, kpos, qslope)


def _outproj_kernel(oa_ref, ob_ref, x_ref, mod_ref, g_ref, w_ref, x1_ref, h2_ref):
    half = oa_ref.shape[2]
    mix = jnp.dot(oa_ref[0], w_ref[:half, :], preferred_element_type=F32)
    mix = mix + jnp.dot(ob_ref[0], w_ref[half:, :], preferred_element_type=F32)
    x1 = x_ref[0] + mod_ref[0, 2:3, :] * mix
    x1_ref[0] = x1
    ms = jnp.mean(x1 * x1, axis=-1, keepdims=True)
    y = x1 * lax.rsqrt(ms + EPS) * g_ref[...]
    h2_ref[0] = (y * (1.0 + mod_ref[0, 4:5, :]) + mod_ref[0, 3:4, :]).astype(BF16)


def _out_proj(oa, ob, x, mod, g_ffn, w_o):
    B, S, D = x.shape
    tm = 512
    half = oa.shape[2]
    return pl.pallas_call(
        _outproj_kernel,
        grid=(B, S // tm),
        in_specs=[pl.BlockSpec((1, tm, half), lambda b, i: (b, i, 0)),
                  pl.BlockSpec((1, tm, half), lambda b, i: (b, i, 0)),
                  pl.BlockSpec((1, tm, D), lambda b, i: (b, i, 0)),
                  pl.BlockSpec((1, N_MOD, D), lambda b, i: (b, 0, 0)),
                  pl.BlockSpec((1, D), lambda b, i: (0, 0)),
                  pl.BlockSpec((2 * half, D), lambda b, i: (0, 0))],
        out_specs=[pl.BlockSpec((1, tm, D), lambda b, i: (b, i, 0)),
                   pl.BlockSpec((1, tm, D), lambda b, i: (b, i, 0))],
        out_shape=[jax.ShapeDtypeStruct((B, S, D), F32),
                   jax.ShapeDtypeStruct((B, S, D), BF16)],
        compiler_params=_cparams(("arbitrary", "arbitrary")),
        name="out_proj",
    )(oa, ob, x, mod, g_ffn.reshape(1, D), w_o)


def _ffn_kernel(h2_ref, x1_ref, mod_ref, wg_ref, wu_ref, cw_ref, cb_ref, wd_ref, gf_ref, o_ref,
                acc_ref, carry_ref, *, tm):
    i = pl.program_id(1)
    j = pl.program_id(2)
    nj = pl.num_programs(2)

    @pl.when(j == 0)
    def _():
        acc_ref[...] = jnp.zeros(acc_ref.shape, F32)

    h2 = h2_ref[0]
    g = jnp.dot(h2, wg_ref[...], preferred_element_type=F32)
    u = jnp.dot(h2, wu_ref[...], preferred_element_type=F32)

    prev = jnp.where(i > 0, carry_ref[j], 0.0)
    carry_ref[j] = g[tm - 8:, :]
    row = lax.broadcasted_iota(jnp.int32, g.shape, 0)
    g1 = jnp.where(row == 0, prev[7:8, :], pltpu.roll(g, 1, axis=0))
    g2 = jnp.where(row == 0, prev[6:7, :], jnp.where(row == 1, prev[7:8, :], pltpu.roll(g, 2, axis=0)))
    gc = cw_ref[0:1, :] * g2 + cw_ref[1:2, :] * g1 + cw_ref[2:3, :] * g + cb_ref[...]
    a = gc / (1.0 + jnp.exp(-gc)) * u
    acc_ref[...] += jnp.dot(a.astype(BF16), wd_ref[...], preferred_element_type=F32)

    @pl.when(j == nj - 1)
    def _():
        x2 = x1_ref[0] + mod_ref[0, 5:6, :] * acc_ref[...]
        ms = jnp.mean(x2 * x2, axis=-1, keepdims=True)
        o_ref[0] = x2 * lax.rsqrt(ms + EPS) * gf_ref[...]


def _conv_ffn(h2, x1, mod, w_gate, w_up, conv_w, conv_b, w_down, g_final):
    B, S, D = x1.shape
    F = w_gate.shape[1]
    tm, tf = 512, 512
    kern = functools.partial(_ffn_kernel, tm=tm)
    return pl.pallas_call(
        kern,
        grid=(B, S // tm, F // tf),
        in_specs=[pl.BlockSpec((1, tm, D), lambda b, i, j: (b, i, 0)),
                  pl.BlockSpec((1, tm, D), lambda b, i, j: (b, i, 0)),
                  pl.BlockSpec((1, N_MOD, D), lambda b, i, j: (b, 0, 0)),
                  pl.BlockSpec((D, tf), lambda b, i, j: (0, j)),
                  pl.BlockSpec((D, tf), lambda b, i, j: (0, j)),
                  pl.BlockSpec((3, tf), lambda b, i, j: (0, j)),
                  pl.BlockSpec((1, tf), lambda b, i, j: (0, j)),
                  pl.BlockSpec((tf, D), lambda b, i, j: (j, 0)),
                  pl.BlockSpec((1, D), lambda b, i, j: (0, 0))],
        out_specs=pl.BlockSpec((1, tm, D), lambda b, i, j: (b, i, 0)),
        out_shape=jax.ShapeDtypeStruct((B, S, D), F32),
        scratch_shapes=[pltpu.VMEM((tm, D), F32),
                        pltpu.VMEM((F // tf, 8, tf), F32)],
        compiler_params=_cparams(("arbitrary", "arbitrary", "arbitrary")),
        name="conv_ffn",
    )(h2, x1, mod, w_gate, w_up, conv_w, conv_b.reshape(1, F), w_down, g_final.reshape(1, D))


def _regroup_w_in(w_in):
    sizes = [1024, 256, 256, 1024, 64, 16, 1024, 256, 256]
    offs = [0]
    for s in sizes:
        offs.append(offs[-1] + s)
    w_t = w_in.T
    qa, ka, va, qi, ki, wi, qb, kb, vb = [w_t[offs[n]:offs[n + 1]] for n in range(9)]
    z64 = jnp.zeros_like(ki)
    big = jnp.concatenate([qa, qi, qb, ka, va, kb, vb, ki, z64, z64, ki], axis=0).astype(BF16)
    tail = jnp.concatenate([wi, jnp.zeros((128 - IDX_HEADS, w_in.shape[0]), w_in.dtype)], axis=0).astype(BF16)
    return big, tail


def kernel(x, c, w_ada, b_ada, g_mix, w_in, sinks, w_o, g_ffn, w_gate, w_up, conv_w, conv_b, w_down, g_final):
    assert w_ada.shape[0] == 1, "the final norm is fused into the (single) layer's FFN kernel"
    mod = _ada_mod(c, w_ada[0], b_ada[0])
    w_big, w_tail = _regroup_w_in(w_in[0])
    proj, tail, (wo_b, wg_b, wu_b, wd_b) = _in_proj(x, mod, g_mix[0], w_big, w_tail,
                                                    (w_o[0], w_gate[0], w_up[0], w_down[0]))
    oa = _dsa_attention(proj, tail)
    ob = _swa_attention(proj, sinks[0])
    x1, h2 = _out_proj(oa, ob, x, mod, g_ffn[0], wo_b)
    return _conv_ffn(h2, x1, mod, wg_b, wu_b, conv_w[0], conv_b[0], wd_b, g_final)
```

```python
import functools

import jax
import jax.numpy as jnp
import numpy as np
from jax import lax
from jax.experimental import pallas as pl
from jax.experimental.pallas import tpu as pltpu

F32 = jnp.float32
BF16 = jnp.bfloat16

HEAD_DIM = 128
N_HEADS_A = 8
N_HEADS_B = 8
N_KV = 2
HEADS_PER_KV = 4
IDX_HEADS = 16
IDX_DIM = 64
TOPK_MAX = 256
WINDOW = 128
N_MOD = 6
EPS = 1e-6
NEG = -1e30
ATTN_SCALE = HEAD_DIM ** -0.5
LOG2E = 1.4426950408889634
IDX_W_SCALE = (IDX_DIM ** -0.5) * (IDX_HEADS ** -0.5)
SLOPES = [2.0 ** (-8.0 * j / 16.0) for j in range(1, 17)]
SLOPES_A = SLOPES[0::2]
SLOPES_B = SLOPES[1::2]

COL_QA, COL_QI, COL_QB = 0, 1024, 2048
COL_KA, COL_VA, COL_KB, COL_VB = 3072, 3328, 3584, 3840
COL_KIA, COL_KIB = 4096, 4224
PROJ_W = 4352

VMEM_LIMIT = 56 * 1024 * 1024

KEY_LO0 = -2139095040
KEY_HI0 = 2139095040
F32_MIN_NORMAL = 1.1754943508222875e-38
F32_MIN_NORMAL_KEY = 0x00800000


def _cparams(sem):
    return pltpu.CompilerParams(dimension_semantics=sem, vmem_limit_bytes=VMEM_LIMIT)


def _ada_kernel(c_ref, w_ref, b_ref, o_ref):
    c = c_ref[...]
    s = c / (1.0 + jnp.exp(-c))
    o_ref[...] = jnp.dot(s.astype(BF16), w_ref[...].astype(BF16),
                         preferred_element_type=F32) + b_ref[...]


def _ada_mod(c, w_ada, b_ada):
    B, D = c.shape
    N = w_ada.shape[1]
    tn = 1024
    cp = jnp.zeros((8, D), F32).at[:B].set(c)
    out = pl.pallas_call(
        _ada_kernel,
        grid=(N // tn,),
        in_specs=[pl.BlockSpec((8, D), lambda j: (0, 0)),
                  pl.BlockSpec((D, tn), lambda j: (0, j)),
                  pl.BlockSpec((1, tn), lambda j: (0, j))],
        out_specs=pl.BlockSpec((8, tn), lambda j: (0, j)),
        out_shape=jax.ShapeDtypeStruct((8, N), F32),
        compiler_params=_cparams(("arbitrary",)),
        name="ada_mod",
    )(cp, w_ada, b_ada.reshape(1, N))
    return out[:B].reshape(B, N_MOD, D)


def _inproj_kernel(x_ref, mod_ref, g_ref, w_ref, wt_ref, *rest, chunk, n_cast):
    cast_in, (p_ref, t_ref), cast_out = rest[:n_cast], rest[n_cast:n_cast + 2], rest[n_cast + 2:]
    for src, dst in zip(cast_in, cast_out):
        dst[...] = src[...].astype(BF16)
    x = x_ref[0]
    ms = jnp.mean(x * x, axis=-1, keepdims=True)
    y = x * lax.rsqrt(ms + EPS) * g_ref[...]
    h = y * (1.0 + mod_ref[0, 1:2, :]) + mod_ref[0, 0:1, :]
    hb = h.astype(BF16)
    dn_nt = (((1,), (1,)), ((), ()))
    for n in range(0, PROJ_W, chunk):
        p_ref[0, :, n:n + chunk] = lax.dot_general(
            hb, w_ref[n:n + chunk, :], dn_nt, preferred_element_type=F32).astype(BF16)
    t_ref[0] = lax.dot_general(hb, wt_ref[...], dn_nt, preferred_element_type=F32)


def _in_proj(x, mod, g_mix, w_big, w_tail, f32_weights):
    B, S, D = x.shape
    tm = 512
    n_steps = B * (S // tm)
    cast_specs = []
    for w in f32_weights:
        rows = w.shape[0] // n_steps
        assert rows * n_steps == w.shape[0] and rows % 16 == 0, w.shape
        cast_specs.append(pl.BlockSpec((rows, w.shape[1]), lambda b, i: (b * (S // tm) + i, 0)))
    outs = pl.pallas_call(
        functools.partial(_inproj_kernel, chunk=256, n_cast=len(f32_weights)),
        grid=(B, S // tm),
        in_specs=[pl.BlockSpec((1, tm, D), lambda b, i: (b, i, 0)),
                  pl.BlockSpec((1, N_MOD, D), lambda b, i: (b, 0, 0)),
                  pl.BlockSpec((1, D), lambda b, i: (0, 0)),
                  pl.BlockSpec((PROJ_W, D), lambda b, i: (0, 0)),
                  pl.BlockSpec((128, D), lambda b, i: (0, 0))] + cast_specs,
        out_specs=[pl.BlockSpec((1, tm, PROJ_W), lambda b, i: (b, i, 0)),
                   pl.BlockSpec((1, tm, 128), lambda b, i: (b, i, 0))] + cast_specs,
        out_shape=[jax.ShapeDtypeStruct((B, S, PROJ_W), BF16),
                   jax.ShapeDtypeStruct((B, S, 128), F32)]
                  + [jax.ShapeDtypeStruct(w.shape, BF16) for w in f32_weights],
        compiler_params=_cparams(("arbitrary", "arbitrary")),
        name="in_proj",
    )(x, mod, g_mix.reshape(1, D), w_big, w_tail, *f32_weights)
    return outs[0], outs[1], outs[2:]


def _key_to_f32(k):
    bits = k ^ ((k >> 31) & 0x7FFFFFFF)
    return lax.bitcast_convert_type(bits, F32)


def _f32_to_key(f):
    bits = lax.bitcast_convert_type(f, jnp.int32)
    return bits ^ ((bits >> 31) & 0x7FFFFFFF)


def _dsa_kernel(qa_ref, qi_ref, w_ref, ka_ref, va_ref, kia_ref, kib_ref, kpos_ref, qslope_ref, o_ref,
                vat_ref, score_ref, wt_ref, smin_ref, smax_ref, lm_ref, p_ref, mb_ref, m_ref, l_ref, acc_ref,
                *, TQ, TK, S, STEPS, CG, CG_IDX):
    i = pl.program_id(1)
    q0 = i * TQ
    n_ch = i + 1
    kf = float(TOPK_MAX)
    dn_nt = (((1,), (1,)), ((), ()))

    key_i = lax.broadcasted_iota(jnp.int32, (TK, TQ), 0)
    qry_i = lax.broadcasted_iota(jnp.int32, (TK, TQ), 1)
    t_idx = q0 + lax.broadcasted_iota(jnp.int32, (1, TQ), 1)

    wt_ref[...] = w_ref[0].T * IDX_W_SCALE

    @pl.when(i == 0)
    def _():
        def transpose_chunk(kc, carry):
            k0 = pl.multiple_of(kc * TK, TK)
            vat_ref[kc] = va_ref[0, pl.ds(k0, TK), :].astype(F32).T.astype(BF16)
            return carry

        lax.fori_loop(0, S // TK, transpose_chunk, 0)

    def chunk_steps(step, cg):
        def group(p, carry):
            step(cg * p, cg)
            return carry

        lax.fori_loop(0, n_ch // cg, group, 0)
        size = cg // 2
        while size >= 1:
            @pl.when((n_ch & size) != 0)
            def _(size=size):
                step((n_ch // (2 * size)) * (2 * size), size)
            size //= 2

    smin_ref[...] = jnp.full(smin_ref.shape, jnp.inf, F32)
    smax_ref[...] = jnp.full(smax_ref.shape, -jnp.inf, F32)

    def idx_step(c0, nc):
        rows = nc * TK
        k0 = pl.multiple_of(c0 * TK, TK)
        kia = kia_ref[0, pl.ds(k0, rows), :]
        kib = kib_ref[0, pl.ds(k0, rows), :]
        acc = jnp.zeros((rows, TQ), F32)
        for j in range(IDX_HEADS // 2):
            qp = qi_ref[0, :, j * 128:(j + 1) * 128]
            sa = lax.dot_general(kia, qp, dn_nt, preferred_element_type=F32)
            sb = lax.dot_general(kib, qp, dn_nt, preferred_element_type=F32)
            acc = (acc + jnp.maximum(sa, 0.0) * wt_ref[2 * j:2 * j + 1, :]
                   + jnp.maximum(sb, 0.0) * wt_ref[2 * j + 1:2 * j + 2, :])
        key_r = lax.broadcasted_iota(jnp.int32, (rows, TQ), 0)
        qry_r = lax.broadcasted_iota(jnp.int32, (rows, TQ), 1)
        causal = (k0 + key_r) <= (q0 + qry_r)
        score_ref[pl.ds(c0, nc)] = jnp.where(causal, acc, -jnp.inf).reshape(nc, TK, TQ)
        smin_ref[...] = jnp.minimum(smin_ref[...],
                                    jnp.min(jnp.where(causal, acc, jnp.inf), axis=0, keepdims=True))
        smax_ref[...] = jnp.maximum(smax_ref[...],
                                    jnp.max(jnp.where(causal, acc, -jnp.inf), axis=0, keepdims=True))

    chunk_steps(idx_step, CG_IDX)
    smin, smax = smin_ref[...], smax_ref[...]

    def count_keys(pred):
        def body(kc, acc):
            m = pred(score_ref[kc], kc)
            return acc + m.reshape(TK // 32, 4, 8, TQ).sum(axis=0)

        acc = lax.fori_loop(0, n_ch, body, jnp.zeros((4, 8, TQ), F32))
        return jnp.sum(acc.sum(axis=0), axis=0, keepdims=True)

    def count_ge(t):
        return count_keys(lambda blk, kc: jnp.where(blk >= t, 1.0, 0.0))

    n_masked = (S - 1 - t_idx).astype(F32)
    n_causal = (t_idx + 1).astype(F32)

    def bis_cond(st):
        return jnp.logical_and(st[4] > 0.5, st[5] < 2 * 32 + 8)

    def bis_body(st):
        lo, hi, cnt_lo, cph_lo, _, it = st
        for step in range(STEPS):
            mid = (lo >> 1) + (hi >> 1) + (lo & hi & 1)
            if step % 2 == 0:
                f_lo, f_hi = _key_to_f32(lo), _key_to_f32(hi)
                mid_v = _f32_to_key(f_lo + (f_hi - f_lo) * 0.5)
                mid = jnp.where(jnp.logical_and(mid_v > lo, mid_v < hi), mid_v, mid)
            fm = _key_to_f32(mid)
            cph = count_ge(fm)
            c = cph + jnp.where(fm <= NEG, n_masked, 0.0)
            ge = c >= kf
            lo = jnp.where(ge, mid, lo)
            hi = jnp.where(ge, hi, mid)
            cnt_lo = jnp.where(ge, c, cnt_lo)
            cph_lo = jnp.where(ge, cph, cph_lo)
        done = jnp.logical_or(cnt_lo == kf, lo + 1 >= hi)
        return lo, hi, cnt_lo, cph_lo, jnp.max(jnp.where(done, 0.0, 1.0)), it + STEPS

    below_neg = smin < NEG
    few = jnp.logical_and(n_causal < kf, jnp.logical_not(below_neg))
    key_neg = _f32_to_key(jnp.full((1, TQ), NEG, F32))
    lo0 = jnp.where(below_neg, KEY_LO0, jnp.where(few, key_neg, _f32_to_key(smin)))
    above_max = jnp.where(jnp.abs(smax) < F32_MIN_NORMAL, F32_MIN_NORMAL_KEY, _f32_to_key(smax) + 1)
    hi0 = jnp.where(below_neg, KEY_HI0, jnp.where(few, key_neg + 1, above_max))
    cnt0 = n_causal + jnp.where(_key_to_f32(lo0) <= NEG, n_masked, 0.0)
    st0 = (lo0, hi0, cnt0, n_causal, jnp.float32(1.0), jnp.int32(0))
    lo, hi, cnt_lo, cph_lo, _, _ = lax.while_loop(bis_cond, bis_body, st0)
    thr = _key_to_f32(lo)

    f_hi = _key_to_f32(hi)
    masked_hi = jnp.where(f_hi <= NEG, n_masked, 0.0)
    need = jnp.logical_and(cnt_lo > kf, cph_lo > kf - masked_hi)
    any_need = jnp.max(jnp.where(need, 1.0, 0.0)) > 0.5

    @pl.when(any_need)
    def _fix_ties():
        slots = kf - count_ge(f_hi) - masked_hi

        def jb(_, st):
            jlo, jhi = st
            jm = (jlo + jhi) >> 1
            ties = count_keys(lambda blk, kc: jnp.where(
                blk == thr, jnp.where(kc * TK + key_i <= jm, 1.0, 0.0), 0.0))
            ge = ties >= slots
            return jnp.where(ge, jlo, jm), jnp.where(ge, jm, jhi)

        n_iter = max(1, (S - 1).bit_length())
        _, jcut = lax.fori_loop(0, n_iter, jb, (jnp.full((1, TQ), -1, jnp.int32),
                                                 jnp.full((1, TQ), S - 1, jnp.int32)))
        jcut = jnp.where(need, jcut, S)

        def drop(kc, carry):
            blk = score_ref[kc]
            surplus = jnp.where(blk == thr, jnp.where(kc * TK + key_i > jcut, 1.0, 0.0), 0.0)
            score_ref[kc] = jnp.where(surplus > 0.5, -jnp.inf, blk)
            return carry

        lax.fori_loop(0, n_ch, drop, 0)

    m_ref[...] = jnp.full(m_ref.shape, NEG, F32)
    l_ref[...] = jnp.zeros(l_ref.shape, F32)
    acc_ref[...] = jnp.zeros(acc_ref.shape, F32)

    def att_step(c0, nc):
        rows = nc * TK
        k0 = pl.multiple_of(c0 * TK, TK)
        mb_ref[:rows] = jnp.where(score_ref[pl.ds(c0, nc)].reshape(rows, TQ) >= thr, 0.0, NEG)
        kpos = kpos_ref[pl.ds(k0, rows), :]
        col_max = []
        for hh in range(N_HEADS_A):
            g = hh // HEADS_PER_KV
            kg = ka_ref[0, pl.ds(k0, rows), g * HEAD_DIM:(g + 1) * HEAD_DIM]
            qh = qa_ref[0, :, hh * HEAD_DIM:(hh + 1) * HEAD_DIM]
            s = lax.dot_general(jnp.concatenate([kg, kpos], axis=1),
                                jnp.concatenate([qh, qslope_ref[hh]], axis=1),
                                dn_nt, preferred_element_type=F32)
            lm = s * (ATTN_SCALE * LOG2E) + mb_ref[:rows]
            lm_ref[hh, :rows] = lm
            col_max.append(jnp.max(lm, axis=0, keepdims=True))
        for hh in range(N_HEADS_A):
            g = hh // HEADS_PER_KV
            vt = jnp.concatenate([vat_ref[c0 + c, g * HEAD_DIM:(g + 1) * HEAD_DIM, :] for c in range(nc)],
                                 axis=1)
            m_old = m_ref[hh]
            m_new = jnp.maximum(m_old, col_max[hh])
            alpha = jnp.exp2(m_old - m_new)
            m_ref[hh] = m_new
            p = jnp.exp2(lm_ref[hh, :rows] - m_new)
            l_ref[hh] = alpha * l_ref[hh] + jnp.sum(p, axis=0, keepdims=True)
            p_ref[hh, :rows] = p.astype(BF16)
            pv = jnp.dot(vt, p_ref[hh, :rows], preferred_element_type=F32)
            acc_ref[hh] = alpha * acc_ref[hh] + pv

    chunk_steps(att_step, CG)

    for hh in range(N_HEADS_A):
        o_t = acc_ref[hh] / l_ref[hh]
        o_ref[0, :, hh * HEAD_DIM:(hh + 1) * HEAD_DIM] = o_t.T.astype(BF16)


POS_RADIX = 64
N_SPLIT = 3


def _alibi_operands(n_pos, TQ, slopes):
    assert n_pos <= POS_RADIX * POS_RADIX
    s = np.arange(n_pos)
    kpos = np.zeros((n_pos, 128), np.float32)
    kpos[:, :N_SPLIT] = (s // POS_RADIX)[:, None]
    kpos[:, N_SPLIT:2 * N_SPLIT] = (s % POS_RADIX)[:, None]
    qslope = np.zeros((len(slopes), 128), np.float32)
    for h, slope in enumerate(slopes):
        rest = np.float32(slope / ATTN_SCALE)
        for n in range(N_SPLIT):
            piece = np.float32(rest.astype(BF16))
            qslope[h, n], qslope[h, N_SPLIT + n] = piece * POS_RADIX, piece
            rest = np.float32(rest - piece)
    qslope = np.broadcast_to(qslope[:, None, :], (len(slopes), TQ, 128))
    return jnp.asarray(kpos.astype(BF16)), jnp.asarray(qslope.astype(BF16))


def _dsa_attention(proj, tail):
    B, S, _ = proj.shape
    TQ = TK = 256
    n_kc = S // TK
    kpos, qslope = _alibi_operands(S, TQ, SLOPES_A)
    CG = 4
    CG_IDX = 4
    kern = functools.partial(_dsa_kernel, TQ=TQ, TK=TK, S=S, STEPS=2, CG=CG, CG_IDX=CG_IDX)
    return pl.pallas_call(
        kern,
        grid=(B, S // TQ),
        in_specs=[pl.BlockSpec((1, TQ, 1024), lambda b, i: (b, i, COL_QA // 1024)),
                  pl.BlockSpec((1, TQ, 1024), lambda b, i: (b, i, COL_QI // 1024)),
                  pl.BlockSpec((1, TQ, 128), lambda b, i: (b, i, 0)),
                  pl.BlockSpec((1, S, 256), lambda b, i: (b, 0, COL_KA // 256)),
                  pl.BlockSpec((1, S, 256), lambda b, i: (b, 0, COL_VA // 256)),
                  pl.BlockSpec((1, S, 128), lambda b, i: (b, 0, COL_KIA // 128)),
                  pl.BlockSpec((1, S, 128), lambda b, i: (b, 0, COL_KIB // 128)),
                  pl.BlockSpec((S, 128), lambda b, i: (0, 0)),
                  pl.BlockSpec((N_HEADS_A, TQ, 128), lambda b, i: (0, 0, 0))],
        out_specs=pl.BlockSpec((1, TQ, 1024), lambda b, i: (b, i, 0)),
        out_shape=jax.ShapeDtypeStruct((B, S, N_HEADS_A * HEAD_DIM), BF16),
        scratch_shapes=[pltpu.VMEM((n_kc, N_KV * HEAD_DIM, TK), BF16),
                        pltpu.VMEM((n_kc, TK, TQ), F32),
                        pltpu.VMEM((128, TQ), F32),
                        pltpu.VMEM((1, TQ), F32),
                        pltpu.VMEM((1, TQ), F32),
                        pltpu.VMEM((N_HEADS_A, CG * TK, TQ), F32),
                        pltpu.VMEM((N_HEADS_A, CG * TK, TQ), BF16),
                        pltpu.VMEM((CG * TK, TQ), F32),
                        pltpu.VMEM((N_HEADS_A, 1, TQ), F32),
                        pltpu.VMEM((N_HEADS_A, 1, TQ), F32),
                        pltpu.VMEM((N_HEADS_A, HEAD_DIM, TQ), F32)],
        compiler_params=_cparams(("arbitrary", "arbitrary")),
        name="dsa_attn",
    )(proj, proj, tail, proj, proj, proj, proj, kpos, qslope)


def _swa_kernel(sink_ref, q_ref, k_ref, v_ref, kpos_ref, qslope_ref, o_ref, lm_ref, p_ref, mb_ref, *, TQ, KW):
    i = pl.program_id(1)
    q0 = i * TQ
    start = pl.multiple_of(jnp.maximum(q0 - WINDOW, 0), WINDOW)
    dn_nt = (((1,), (1,)), ((), ()))
    key_r = lax.broadcasted_iota(jnp.int32, (KW, TQ), 0)
    qry_r = lax.broadcasted_iota(jnp.int32, (KW, TQ), 1)
    dist = (q0 + qry_r) - (start + key_r)
    mb_ref[...] = jnp.where(dist >= 0, jnp.where(dist < WINDOW, 0.0, NEG), NEG)
    t_rel = (q0 - start + lax.broadcasted_iota(jnp.int32, (1, TQ), 1)).astype(F32)
    kpos = kpos_ref[...]
    for hh in range(N_HEADS_B):
        g = hh // HEADS_PER_KV
        kg = k_ref[0, pl.ds(start, KW), g * HEAD_DIM:(g + 1) * HEAD_DIM]
        qh = q_ref[0, :, hh * HEAD_DIM:(hh + 1) * HEAD_DIM]
        lm_ref[hh] = lax.dot_general(jnp.concatenate([kg, kpos], axis=1),
                                     jnp.concatenate([qh, qslope_ref[hh]], axis=1),
                                     dn_nt, preferred_element_type=F32)
    vts = [v_ref[0, pl.ds(start, KW), g * HEAD_DIM:(g + 1) * HEAD_DIM].astype(F32).T.astype(BF16)
           for g in range(N_KV)]
    for hh in range(N_HEADS_B):
        lm = lm_ref[hh] * (ATTN_SCALE * LOG2E) + mb_ref[...]
        sink = (sink_ref[hh] + SLOPES_B[hh] * t_rel) * LOG2E
        m = jnp.maximum(jnp.max(lm, axis=0, keepdims=True), sink)
        p = jnp.exp2(lm - m)
        den = jnp.sum(p, axis=0, keepdims=True) + jnp.exp2(sink - m)
        p_ref[hh] = p.astype(BF16)
        pv = jnp.dot(vts[hh // HEADS_PER_KV], p_ref[hh], preferred_element_type=F32)
        o_ref[0, :, hh * HEAD_DIM:(hh + 1) * HEAD_DIM] = (pv / den).T.astype(BF16)


def _swa_attention(proj, sinks):
    B, S, _ = proj.shape
    TQ = 256
    KW = TQ + WINDOW
    kpos, qslope = _alibi_operands(KW, TQ, SLOPES_B)
    kern = functools.partial(_swa_kernel, TQ=TQ, KW=KW)
    return pl.pallas_call(
        kern,
        grid=(B, S // TQ),
        in_specs=[pl.BlockSpec(memory_space=pltpu.SMEM),
                  pl.BlockSpec((1, TQ, 1024), lambda b, i: (b, i, COL_QB // 1024)),
                  pl.BlockSpec((1, S, 256), lambda b, i: (b, 0, COL_KB // 256)),
                  pl.BlockSpec((1, S, 256), lambda b, i: (b, 0, COL_VB // 256)),
                  pl.BlockSpec((KW, 128), lambda b, i: (0, 0)),
                  pl.BlockSpec((N_HEADS_B, TQ, 128), lambda b, i: (0, 0, 0))],
        out_specs=pl.BlockSpec((1, TQ, 1024), lambda b, i: (b, i, 0)),
        out_shape=jax.ShapeDtypeStruct((B, S, N_HEADS_B * HEAD_DIM), BF16),
        scratch_shapes=[pltpu.VMEM((N_HEADS_B, KW, TQ), F32),
                        pltpu.VMEM((N_HEADS_B, KW, TQ), BF16),
                        pltpu.VMEM((KW, TQ), F32)],
        compiler_params=_cparams(("arbitrary", "arbitrary")),
        name="swa_attn",
    )(sinks, proj, proj, proj, kpos, qslope)


def _outproj_kernel(oa_ref, ob_ref, x_ref, mod_ref, g_ref, w_ref, x1_ref, h2_ref):
    half = oa_ref.shape[2]
    mix = jnp.dot(oa_ref[0], w_ref[:half, :], preferred_element_type=F32)
    mix = mix + jnp.dot(ob_ref[0], w_ref[half:, :], preferred_element_type=F32)
    x1 = x_ref[0] + mod_ref[0, 2:3, :] * mix
    x1_ref[0] = x1
    ms = jnp.mean(x1 * x1, axis=-1, keepdims=True)
    y = x1 * lax.rsqrt(ms + EPS) * g_ref[...]
    h2_ref[0] = (y * (1.0 + mod_ref[0, 4:5, :]) + mod_ref[0, 3:4, :]).astype(BF16)


def _out_proj(oa, ob, x, mod, g_ffn, w_o):
    B, S, D = x.shape
    tm = 512
    half = oa.shape[2]
    return pl.pallas_call(
        _outproj_kernel,
        grid=(B, S // tm),
        in_specs=[pl.BlockSpec((1, tm, half), lambda b, i: (b, i, 0)),
                  pl.BlockSpec((1, tm, half), lambda b, i: (b, i, 0)),
                  pl.BlockSpec((1, tm, D), lambda b, i: (b, i, 0)),
                  pl.BlockSpec((1, N_MOD, D), lambda b, i: (b, 0, 0)),
                  pl.BlockSpec((1, D), lambda b, i: (0, 0)),
                  pl.BlockSpec((2 * half, D), lambda b, i: (0, 0))],
        out_specs=[pl.BlockSpec((1, tm, D), lambda b, i: (b, i, 0)),
                   pl.BlockSpec((1, tm, D), lambda b, i: (b, i, 0))],
        out_shape=[jax.ShapeDtypeStruct((B, S, D), F32),
                   jax.ShapeDtypeStruct((B, S, D), BF16)],
        compiler_params=_cparams(("arbitrary", "arbitrary")),
        name="out_proj",
    )(oa, ob, x, mod, g_ffn.reshape(1, D), w_o)


def _ffn_kernel(h2_ref, x1_ref, mod_ref, wg_ref, wu_ref, cw_ref, cb_ref, wd_ref, gf_ref, o_ref,
                acc_ref, carry_ref, *, tm):
    i = pl.program_id(1)
    j = pl.program_id(2)
    nj = pl.num_programs(2)

    @pl.when(j == 0)
    def _():
        acc_ref[...] = jnp.zeros(acc_ref.shape, F32)

    h2 = h2_ref[0]
    g = jnp.dot(h2, wg_ref[...], preferred_element_type=F32)
    u = jnp.dot(h2, wu_ref[...], preferred_element_type=F32)

    prev = jnp.where(i > 0, carry_ref[j], 0.0)
    carry_ref[j] = g[tm - 8:, :]
    row = lax.broadcasted_iota(jnp.int32, g.shape, 0)
    g1 = jnp.where(row == 0, prev[7:8, :], pltpu.roll(g, 1, axis=0))
    g2 = jnp.where(row == 0, prev[6:7, :], jnp.where(row == 1, prev[7:8, :], pltpu.roll(g, 2, axis=0)))
    gc = cw_ref[0:1, :] * g2 + cw_ref[1:2, :] * g1 + cw_ref[2:3, :] * g + cb_ref[...]
    a = gc / (1.0 + jnp.exp(-gc)) * u
    acc_ref[...] += jnp.dot(a.astype(BF16), wd_ref[...], preferred_element_type=F32)

    @pl.when(j == nj - 1)
    def _():
        x2 = x1_ref[0] + mod_ref[0, 5:6, :] * acc_ref[...]
        ms = jnp.mean(x2 * x2, axis=-1, keepdims=True)
        o_ref[0] = x2 * lax.rsqrt(ms + EPS) * gf_ref[...]


def _conv_ffn(h2, x1, mod, w_gate, w_up, conv_w, conv_b, w_down, g_final):
    B, S, D = x1.shape
    F = w_gate.shape[1]
    tm, tf = 512, 512
    kern = functools.partial(_ffn_kernel, tm=tm)
    return pl.pallas_call(
        kern,
        grid=(B, S // tm, F // tf),
        in_specs=[pl.BlockSpec((1, tm, D), lambda b, i, j: (b, i, 0)),
                  pl.BlockSpec((1, tm, D), lambda b, i, j: (b, i, 0)),
                  pl.BlockSpec((1, N_MOD, D), lambda b, i, j: (b, 0, 0)),
                  pl.BlockSpec((D, tf), lambda b, i, j: (0, j)),
                  pl.BlockSpec((D, tf), lambda b, i, j: (0, j)),
                  pl.BlockSpec((3, tf), lambda b, i, j: (0, j)),
                  pl.BlockSpec((1, tf), lambda b, i, j: (0, j)),
                  pl.BlockSpec((tf, D), lambda b, i, j: (j, 0)),
                  pl.BlockSpec((1, D), lambda b, i, j: (0, 0))],
        out_specs=pl.BlockSpec((1, tm, D), lambda b, i, j: (b, i, 0)),
        out_shape=jax.ShapeDtypeStruct((B, S, D), F32),
        scratch_shapes=[pltpu.VMEM((tm, D), F32),
                        pltpu.VMEM((F // tf, 8, tf), F32)],
        compiler_params=_cparams(("arbitrary", "arbitrary", "arbitrary")),
        name="conv_ffn",
    )(h2, x1, mod, w_gate, w_up, conv_w, conv_b.reshape(1, F), w_down, g_final.reshape(1, D))


def _regroup_w_in(w_in):
    sizes = [1024, 256, 256, 1024, 64, 16, 1024, 256, 256]
    offs = [0]
    for s in sizes:
        offs.append(offs[-1] + s)
    w_t = w_in.T
    qa, ka, va, qi, ki, wi, qb, kb, vb = [w_t[offs[n]:offs[n + 1]] for n in range(9)]
    z64 = jnp.zeros_like(ki)
    big = jnp.concatenate([qa, qi, qb, ka, va, kb, vb, ki, z64, z64, ki], axis=0).astype(BF16)
    tail = jnp.concatenate([wi, jnp.zeros((128 - IDX_HEADS, w_in.shape[0]), w_in.dtype)], axis=0).astype(BF16)
    return big, tail


def kernel(x, c, w_ada, b_ada, g_mix, w_in, sinks, w_o, g_ffn, w_gate, w_up, conv_w, conv_b, w_down, g_final):
    assert w_ada.shape[0] == 1, "the final norm is fused into the (single) layer's FFN kernel"
    mod = _ada_mod(c, w_ada[0], b_ada[0])
    w_big, w_tail = _regroup_w_in(w_in[0])
    proj, tail, (wo_b, wg_b, wu_b, wd_b) = _in_proj(x, mod, g_mix[0], w_big, w_tail,
                                                    (w_o[0], w_gate[0], w_up[0], w_down[0]))
    oa = _dsa_attention(proj, tail)
    ob = _swa_attention(proj, sinks[0])
    x1, h2 = _out_proj(oa, ob, x, mod, g_ffn[0], wo_b)
    return _conv_ffn(h2, x1, mod, wg_b, wu_b, conv_w[0], conv_b[0], wd_b, g_final)
```

```python
import functools

import jax
import jax.numpy as jnp
import numpy as np
from jax import lax
from jax.experimental import pallas as pl
from jax.experimental.pallas import tpu as pltpu

F32 = jnp.float32
BF16 = jnp.bfloat16

HEAD_DIM = 128
N_HEADS_A = 8
N_HEADS_B = 8
N_KV = 2
HEADS_PER_KV = 4
IDX_HEADS = 16
IDX_DIM = 64
TOPK_MAX = 256
WINDOW = 128
N_MOD = 6
EPS = 1e-6
NEG = -1e30
ATTN_SCALE = HEAD_DIM ** -0.5
LOG2E = 1.4426950408889634
IDX_W_SCALE = (IDX_DIM ** -0.5) * (IDX_HEADS ** -0.5)
SLOPES = [2.0 ** (-8.0 * j / 16.0) for j in range(1, 17)]
SLOPES_A = SLOPES[0::2]
SLOPES_B = SLOPES[1::2]

COL_QA, COL_QI, COL_QB = 0, 1024, 2048
COL_KA, COL_VA, COL_KB, COL_VB = 3072, 3328, 3584, 3840
COL_KIA, COL_KIB = 4096, 4224
PROJ_W = 4352

VMEM_LIMIT = 56 * 1024 * 1024

KEY_LO0 = -2139095040
KEY_HI0 = 2139095040
F32_MIN_NORMAL = 1.1754943508222875e-38
F32_MIN_NORMAL_KEY = 0x00800000


def _cparams(sem):
    return pltpu.CompilerParams(dimension_semantics=sem, vmem_limit_bytes=VMEM_LIMIT)


def _ada_kernel(c_ref, w_ref, b_ref, o_ref):
    c = c_ref[...]
    s = c / (1.0 + jnp.exp(-c))
    o_ref[...] = jnp.dot(s.astype(BF16), w_ref[...].astype(BF16),
                         preferred_element_type=F32) + b_ref[...]


def _ada_mod(c, w_ada, b_ada):
    B, D = c.shape
    N = w_ada.shape[1]
    tn = 1024
    cp = jnp.zeros((8, D), F32).at[:B].set(c)
    out = pl.pallas_call(
        _ada_kernel,
        grid=(N // tn,),
        in_specs=[pl.BlockSpec((8, D), lambda j: (0, 0)),
                  pl.BlockSpec((D, tn), lambda j: (0, j)),
                  pl.BlockSpec((1, tn), lambda j: (0, j))],
        out_specs=pl.BlockSpec((8, tn), lambda j: (0, j)),
        out_shape=jax.ShapeDtypeStruct((8, N), F32),
        compiler_params=_cparams(("arbitrary",)),
        name="ada_mod",
    )(cp, w_ada, b_ada.reshape(1, N))
    return out[:B].reshape(B, N_MOD, D)


def _inproj_kernel(x_ref, mod_ref, g_ref, w_ref, wt_ref, *rest, chunk, n_cast):
    cast_in, (p_ref, t_ref), cast_out = rest[:n_cast], rest[n_cast:n_cast + 2], rest[n_cast + 2:]
    for src, dst in zip(cast_in, cast_out):
        dst[...] = src[...].astype(BF16)
    x = x_ref[0]
    ms = jnp.mean(x * x, axis=-1, keepdims=True)
    y = x * lax.rsqrt(ms + EPS) * g_ref[...]
    h = y * (1.0 + mod_ref[0, 1:2, :]) + mod_ref[0, 0:1, :]
    hb = h.astype(BF16)
    dn_nt = (((1,), (1,)), ((), ()))
    for n in range(0, PROJ_W, chunk):
        p_ref[0, :, n:n + chunk] = lax.dot_general(
            hb, w_ref[n:n + chunk, :], dn_nt, preferred_element_type=F32).astype(BF16)
    t_ref[0] = lax.dot_general(hb, wt_ref[...], dn_nt, preferred_element_type=F32)


def _in_proj(x, mod, g_mix, w_big, w_tail, f32_weights):
    B, S, D = x.shape
    tm = 512
    n_steps = B * (S // tm)
    cast_specs = []
    for w in f32_weights:
        rows = w.shape[0] // n_steps
        assert rows * n_steps == w.shape[0] and rows % 16 == 0, w.shape
        cast_specs.append(pl.BlockSpec((rows, w.shape[1]), lambda b, i: (b * (S // tm) + i, 0)))
    outs = pl.pallas_call(
        functools.partial(_inproj_kernel, chunk=256, n_cast=len(f32_weights)),
        grid=(B, S // tm),
        in_specs=[pl.BlockSpec((1, tm, D), lambda b, i: (b, i, 0)),
                  pl.BlockSpec((1, N_MOD, D), lambda b, i: (b, 0, 0)),
                  pl.BlockSpec((1, D), lambda b, i: (0, 0)),
                  pl.BlockSpec((PROJ_W, D), lambda b, i: (0, 0)),
                  pl.BlockSpec((128, D), lambda b, i: (0, 0))] + cast_specs,
        out_specs=[pl.BlockSpec((1, tm, PROJ_W), lambda b, i: (b, i, 0)),
                   pl.BlockSpec((1, tm, 128), lambda b, i: (b, i, 0))] + cast_specs,
        out_shape=[jax.ShapeDtypeStruct((B, S, PROJ_W), BF16),
                   jax.ShapeDtypeStruct((B, S, 128), F32)]
                  + [jax.ShapeDtypeStruct(w.shape, BF16) for w in f32_weights],
        compiler_params=_cparams(("arbitrary", "arbitrary")),
        name="in_proj",
    )(x, mod, g_mix.reshape(1, D), w_big, w_tail, *f32_weights)
    return outs[0], outs[1], outs[2:]


def _key_to_f32(k):
    bits = k ^ ((k >> 31) & 0x7FFFFFFF)
    return lax.bitcast_convert_type(bits, F32)


def _f32_to_key(f):
    bits = lax.bitcast_convert_type(f, jnp.int32)
    return bits ^ ((bits >> 31) & 0x7FFFFFFF)


def _dsa_kernel(qa_ref, qi_ref, w_ref, ka_ref, va_ref, kia_ref, kib_ref, kpos_ref, qslope_ref, o_ref,
                vat_ref, score_ref, wt_ref, smin_ref, smax_ref, lm_ref, p_ref, mb_ref, m_ref, l_ref, acc_ref,
                *, TQ, TK, S, STEPS, CG, CG_IDX):
    i = pl.program_id(1)
    q0 = i * TQ
    n_ch = i + 1
    kf = float(TOPK_MAX)
    dn_nt = (((1,), (1,)), ((), ()))

    key_i = lax.broadcasted_iota(jnp.int32, (TK, TQ), 0)
    qry_i = lax.broadcasted_iota(jnp.int32, (TK, TQ), 1)
    t_idx = q0 + lax.broadcasted_iota(jnp.int32, (1, TQ), 1)

    wt_ref[...] = w_ref[0].T * IDX_W_SCALE

    @pl.when(i == 0)
    def _():
        def transpose_chunk(kc, carry):
            k0 = pl.multiple_of(kc * TK, TK)
            vat_ref[kc] = va_ref[0, pl.ds(k0, TK), :].astype(F32).T.astype(BF16)
            return carry

        lax.fori_loop(0, S // TK, transpose_chunk, 0)

    def chunk_steps(step, cg):
        def group(p, carry):
            step(cg * p, cg)
            return carry

        lax.fori_loop(0, n_ch // cg, group, 0)
        size = cg // 2
        while size >= 1:
            @pl.when((n_ch & size) != 0)
            def _(size=size):
                step((n_ch // (2 * size)) * (2 * size), size)
            size //= 2

    smin_ref[...] = jnp.full(smin_ref.shape, jnp.inf, F32)
    smax_ref[...] = jnp.full(smax_ref.shape, -jnp.inf, F32)

    def idx_step(c0, nc):
        rows = nc * TK
        k0 = pl.multiple_of(c0 * TK, TK)
        kia = kia_ref[0, pl.ds(k0, rows), :]
        kib = kib_ref[0, pl.ds(k0, rows), :]
        acc = jnp.zeros((rows, TQ), F32)
        for j in range(IDX_HEADS // 2):
            qp = qi_ref[0, :, j * 128:(j + 1) * 128]
            sa = lax.dot_general(kia, qp, dn_nt, preferred_element_type=F32)
            sb = lax.dot_general(kib, qp, dn_nt, preferred_element_type=F32)
            acc = (acc + jnp.maximum(sa, 0.0) * wt_ref[2 * j:2 * j + 1, :]
                   + jnp.maximum(sb, 0.0) * wt_ref[2 * j + 1:2 * j + 2, :])
        key_r = lax.broadcasted_iota(jnp.int32, (rows, TQ), 0)
        qry_r = lax.broadcasted_iota(jnp.int32, (rows, TQ), 1)
        causal = (k0 + key_r) <= (q0 + qry_r)
        score_ref[pl.ds(c0, nc)] = jnp.where(causal, acc, -jnp.inf).reshape(nc, TK, TQ)
        smin_ref[...] = jnp.minimum(smin_ref[...],
                                    jnp.min(jnp.where(causal, acc, jnp.inf), axis=0, keepdims=True))
        smax_ref[...] = jnp.maximum(smax_ref[...],
                                    jnp.max(jnp.where(causal, acc, -jnp.inf), axis=0, keepdims=True))

    chunk_steps(idx_step, CG_IDX)
    smin, smax = smin_ref[...], smax_ref[...]

    def count_keys(pred):
        def body(kc, acc):
            m = pred(score_ref[kc], kc)
            return acc + m.reshape(TK // 32, 4, 8, TQ).sum(axis=0)

        acc = lax.fori_loop(0, n_ch, body, jnp.zeros((4, 8, TQ), F32))
        return jnp.sum(acc.sum(axis=0), axis=0, keepdims=True)

    def count_ge(t):
        return count_keys(lambda blk, kc: jnp.where(blk >= t, 1.0, 0.0))

    n_masked = (S - 1 - t_idx).astype(F32)
    n_causal = (t_idx + 1).astype(F32)

    def bis_cond(st):
        return jnp.logical_and(st[4] > 0.5, st[5] < 2 * 32 + 8)

    def bis_body(st):
        lo, hi, cnt_lo, cph_lo, _, it = st
        for step in range(STEPS):
            mid = (lo >> 1) + (hi >> 1) + (lo & hi & 1)
            if step % 2 == 0:
                f_lo, f_hi = _key_to_f32(lo), _key_to_f32(hi)
                mid_v = _f32_to_key(f_lo + (f_hi - f_lo) * 0.5)
                mid = jnp.where(jnp.logical_and(mid_v > lo, mid_v < hi), mid_v, mid)
            fm = _key_to_f32(mid)
            cph = count_ge(fm)
            c = cph + jnp.where(fm <= NEG, n_masked, 0.0)
            ge = c >= kf
            lo = jnp.where(ge, mid, lo)
            hi = jnp.where(ge, hi, mid)
            cnt_lo = jnp.where(ge, c, cnt_lo)
            cph_lo = jnp.where(ge, cph, cph_lo)
        done = jnp.logical_or(cnt_lo == kf, lo + 1 >= hi)
        return lo, hi, cnt_lo, cph_lo, jnp.max(jnp.where(done, 0.0, 1.0)), it + STEPS

    below_neg = smin < NEG
    few = jnp.logical_and(n_causal < kf, jnp.logical_not(below_neg))
    key_neg = _f32_to_key(jnp.full((1, TQ), NEG, F32))
    lo0 = jnp.where(below_neg, KEY_LO0, jnp.where(few, key_neg, _f32_to_key(smin)))
    above_max = jnp.where(jnp.abs(smax) < F32_MIN_NORMAL, F32_MIN_NORMAL_KEY, _f32_to_key(smax) + 1)
    hi0 = jnp.where(below_neg, KEY_HI0, jnp.where(few, key_neg + 1, above_max))
    cnt0 = n_causal + jnp.where(_key_to_f32(lo0) <= NEG, n_masked, 0.0)
    st0 = (lo0, hi0, cnt0, n_causal, jnp.float32(1.0), jnp.int32(0))
    lo, hi, cnt_lo, cph_lo, _, _ = lax.while_loop(bis_cond, bis_body, st0)
    thr = _key_to_f32(lo)

    f_hi = _key_to_f32(hi)
    masked_hi = jnp.where(f_hi <= NEG, n_masked, 0.0)
    need = jnp.logical_and(cnt_lo > kf, cph_lo > kf - masked_hi)
    any_need = jnp.max(jnp.where(need, 1.0, 0.0)) > 0.5

    @pl.when(any_need)
    def _fix_ties():
        slots = kf - count_ge(f_hi) - masked_hi

        def jb(_, st):
            jlo, jhi = st
            jm = (jlo + jhi) >> 1
            ties = count_keys(lambda blk, kc: jnp.where(
                blk == thr, jnp.where(kc * TK + key_i <= jm, 1.0, 0.0), 0.0))
            ge = ties >= slots
            return jnp.where(ge, jlo, jm), jnp.where(ge, jm, jhi)

        n_iter = max(1, (S - 1).bit_length())
        _, jcut = lax.fori_loop(0, n_iter, jb, (jnp.full((1, TQ), -1, jnp.int32),
                                                 jnp.full((1, TQ), S - 1, jnp.int32)))
        jcut = jnp.where(need, jcut, S)

        def drop(kc, carry):
            blk = score_ref[kc]
            surplus = jnp.where(blk == thr, jnp.where(kc * TK + key_i > jcut, 1.0, 0.0), 0.0)
            score_ref[kc] = jnp.where(surplus > 0.5, -jnp.inf, blk)
            return carry

        lax.fori_loop(0, n_ch, drop, 0)

    m_ref[...] = jnp.full(m_ref.shape, NEG, F32)
    l_ref[...] = jnp.zeros(l_ref.shape, F32)
    acc_ref[...] = jnp.zeros(acc_ref.shape, F32)

    def att_step(c0, nc):
        rows = nc * TK
        k0 = pl.multiple_of(c0 * TK, TK)
        mb_ref[:rows] = jnp.where(score_ref[pl.ds(c0, nc)].reshape(rows, TQ) >= thr, 0.0, NEG)
        kpos = kpos_ref[pl.ds(k0, rows), :]
        col_max = []
        for hh in range(N_HEADS_A):
            g = hh // HEADS_PER_KV
            kg = ka_ref[0, pl.ds(k0, rows), g * HEAD_DIM:(g + 1) * HEAD_DIM]
            qh = qa_ref[0, :, hh * HEAD_DIM:(hh + 1) * HEAD_DIM]
            s = lax.dot_general(jnp.concatenate([kg, kpos], axis=1),
                                jnp.concatenate([qh, qslope_ref[hh]], axis=1),
                                dn_nt, preferred_element_type=F32)
            lm = s * (ATTN_SCALE * LOG2E) + mb_ref[:rows]
            lm_ref[hh, :rows] = lm
            col_max.append(jnp.max(lm, axis=0, keepdims=True))
        for hh in range(N_HEADS_A):
            g = hh // HEADS_PER_KV
            vt = jnp.concatenate([vat_ref[c0 + c, g * HEAD_DIM:(g + 1) * HEAD_DIM, :] for c in range(nc)],
                                 axis=1)
            m_old = m_ref[hh]
            m_new = jnp.maximum(m_old, col_max[hh])
            alpha = jnp.exp2(m_old - m_new)
            m_ref[hh] = m_new
            p = jnp.exp2(lm_ref[hh, :rows] - m_new)
            l_ref[hh] = alpha * l_ref[hh] + jnp.sum(p, axis=0, keepdims=True)
            p_ref[hh, :rows] = p.astype(BF16)
            pv = jnp.dot(vt, p_ref[hh, :rows], preferred_element_type=F32)
            acc_ref[hh] = alpha * acc_ref[hh] + pv

    chunk_steps(att_step, CG)

    for hh in range(N_HEADS_A):
        o_t = acc_ref[hh] / l_ref[hh]
        o_ref[0, :, hh * HEAD_DIM:(hh + 1) * HEAD_DIM] = o_t.T.astype(BF16)


POS_RADIX = 64
N_SPLIT = 3


def _alibi_operands(n_pos, TQ, slopes):
    assert n_pos <= POS_RADIX * POS_RADIX
    s = np.arange(n_pos)
    kpos = np.zeros((n_pos, 128), np.float32)
    kpos[:, :N_SPLIT] = (s // POS_RADIX)[:, None]
    kpos[:, N_SPLIT:2 * N_SPLIT] = (s % POS_RADIX)[:, None]
    qslope = np.zeros((len(slopes), 128), np.float32)
    for h, slope in enumerate(slopes):
        rest = np.float32(slope / ATTN_SCALE)
        for n in range(N_SPLIT):
            piece = np.float32(rest.astype(BF16))
            qslope[h, n], qslope[h, N_SPLIT + n] = piece * POS_RADIX, piece
            rest = np.float32(rest - piece)
    qslope = np.broadcast_to(qslope[:, None, :], (len(slopes), TQ, 128))
    return jnp.asarray(kpos.astype(BF16)), jnp.asarray(qslope.astype(BF16))


def _dsa_attention(proj, tail):
    B, S, _ = proj.shape
    TQ = TK = 256
    n_kc = S // TK
    kpos, qslope = _alibi_operands(S, TQ, SLOPES_A)
    CG = 4
    CG_IDX = 8
    kern = functools.partial(_dsa_kernel, TQ=TQ, TK=TK, S=S, STEPS=2, CG=CG, CG_IDX=CG_IDX)
    return pl.pallas_call(
        kern,
        grid=(B, S // TQ),
        in_specs=[pl.BlockSpec((1, TQ, 1024), lambda b, i: (b, i, COL_QA // 1024)),
                  pl.BlockSpec((1, TQ, 1024), lambda b, i: (b, i, COL_QI // 1024)),
                  pl.BlockSpec((1, TQ, 128), lambda b, i: (b, i, 0)),
                  pl.BlockSpec((1, S, 256), lambda b, i: (b, 0, COL_KA // 256)),
                  pl.BlockSpec((1, S, 256), lambda b, i: (b, 0, COL_VA // 256)),
                  pl.BlockSpec((1, S, 128), lambda b, i: (b, 0, COL_KIA // 128)),
                  pl.BlockSpec((1, S, 128), lambda b, i: (b, 0, COL_KIB // 128)),
                  pl.BlockSpec((S, 128), lambda b, i: (0, 0)),
                  pl.BlockSpec((N_HEADS_A, TQ, 128), lambda b, i: (0, 0, 0))],
        out_specs=pl.BlockSpec((1, TQ, 1024), lambda b, i: (b, i, 0)),
        out_shape=jax.ShapeDtypeStruct((B, S, N_HEADS_A * HEAD_DIM), BF16),
        scratch_shapes=[pltpu.VMEM((n_kc, N_KV * HEAD_DIM, TK), BF16),
                        pltpu.VMEM((n_kc, TK, TQ), F32),
                        pltpu.VMEM((128, TQ), F32),
                        pltpu.VMEM((1, TQ), F32),
                        pltpu.VMEM((1, TQ), F32),
                        pltpu.VMEM((N_HEADS_A, CG * TK, TQ), F32),
                        pltpu.VMEM((N_HEADS_A, CG * TK, TQ), BF16),
                        pltpu.VMEM((CG * TK, TQ), F32),
                        pltpu.VMEM((N_HEADS_A, 1, TQ), F32),
                        pltpu.VMEM((N_HEADS_A, 1, TQ), F32),
                        pltpu.VMEM((N_HEADS_A, HEAD_DIM, TQ), F32)],
        compiler_params=_cparams(("arbitrary", "arbitrary")),
        name="dsa_attn",
    )(proj, proj, tail, proj, proj, proj, proj, kpos, qslope)


def _swa_kernel(sink_ref, q_ref, k_ref, v_ref, kpos_ref, qslope_ref, o_ref, lm_ref, p_ref, mb_ref, *, TQ, KW):
    i = pl.program_id(1)
    q0 = i * TQ
    start = pl.multiple_of(jnp.maximum(q0 - WINDOW, 0), WINDOW)
    dn_nt = (((1,), (1,)), ((), ()))
    key_r = lax.broadcasted_iota(jnp.int32, (KW, TQ), 0)
    qry_r = lax.broadcasted_iota(jnp.int32, (KW, TQ), 1)
    dist = (q0 + qry_r) - (start + key_r)
    mb_ref[...] = jnp.where(dist >= 0, jnp.where(dist < WINDOW, 0.0, NEG), NEG)
    t_rel = (q0 - start + lax.broadcasted_iota(jnp.int32, (1, TQ), 1)).astype(F32)
    kpos = kpos_ref[...]
    for hh in range(N_HEADS_B):
        g = hh // HEADS_PER_KV
        kg = k_ref[0, pl.ds(start, KW), g * HEAD_DIM:(g + 1) * HEAD_DIM]
        qh = q_ref[0, :, hh * HEAD_DIM:(hh + 1) * HEAD_DIM]
        lm_ref[hh] = lax.dot_general(jnp.concatenate([kg, kpos], axis=1),
                                     jnp.concatenate([qh, qslope_ref[hh]], axis=1),
                                     dn_nt, preferred_element_type=F32)
    vts = [v_ref[0, pl.ds(start, KW), g * HEAD_DIM:(g + 1) * HEAD_DIM].astype(F32).T.astype(BF16)
           for g in range(N_KV)]
    for hh in range(N_HEADS_B):
        lm = lm_ref[hh] * (ATTN_SCALE * LOG2E) + mb_ref[...]
        sink = (sink_ref[hh] + SLOPES_B[hh] * t_rel) * LOG2E
        m = jnp.maximum(jnp.max(lm, axis=0, keepdims=True), sink)
        p = jnp.exp2(lm - m)
        den = jnp.sum(p, axis=0, keepdims=True) + jnp.exp2(sink - m)
        p_ref[hh] = p.astype(BF16)
        pv = jnp.dot(vts[hh // HEADS_PER_KV], p_ref[hh], preferred_element_type=F32)
        o_ref[0, :, hh * HEAD_DIM:(hh + 1) * HEAD_DIM] = (pv / den).T.astype(BF16)


def _swa_attention(proj, sinks):
    B, S, _ = proj.shape
    TQ = 256
    KW = TQ + WINDOW
    kpos, qslope = _alibi_operands(KW, TQ, SLOPES_B)
    kern = functools.partial(_swa_kernel, TQ=TQ, KW=KW)
    return pl.pallas_call(
        kern,
        grid=(B, S // TQ),
        in_specs=[pl.BlockSpec(memory_space=pltpu.SMEM),
                  pl.BlockSpec((1, TQ, 1024), lambda b, i: (b, i, COL_QB // 1024)),
                  pl.BlockSpec((1, S, 256), lambda b, i: (b, 0, COL_KB // 256)),
                  pl.BlockSpec((1, S, 256), lambda b, i: (b, 0, COL_VB // 256)),
                  pl.BlockSpec((KW, 128), lambda b, i: (0, 0)),
                  pl.BlockSpec((N_HEADS_B, TQ, 128), lambda b, i: (0, 0, 0))],
        out_specs=pl.BlockSpec((1, TQ, 1024), lambda b, i: (b, i, 0)),
        out_shape=jax.ShapeDtypeStruct((B, S, N_HEADS_B * HEAD_DIM), BF16),
        scratch_shapes=[pltpu.VMEM((N_HEADS_B, KW, TQ), F32),
                        pltpu.VMEM((N_HEADS_B, KW, TQ), BF16),
                        pltpu.VMEM((KW, TQ), F32)],
        compiler_params=_cparams(("arbitrary", "arbitrary")),
        name="swa_attn",
    )(sinks, proj, proj, proj, kpos, qslope)


def _outproj_kernel(oa_ref, ob_ref, x_ref, mod_ref, g_ref, w_ref, x1_ref, h2_ref):
    half = oa_ref.shape[2]
    mix = jnp.dot(oa_ref[0], w_ref[:half, :], preferred_element_type=F32)
    mix = mix + jnp.dot(ob_ref[0], w_ref[half:, :], preferred_element_type=F32)
    x1 = x_ref[0] + mod_ref[0, 2:3, :] * mix
    x1_ref[0] = x1
    ms = jnp.mean(x1 * x1, axis=-1, keepdims=True)
    y = x1 * lax.rsqrt(ms + EPS) * g_ref[...]
    h2_ref[0] = (y * (1.0 + mod_ref[0, 4:5, :]) + mod_ref[0, 3:4, :]).astype(BF16)


def _out_proj(oa, ob, x, mod, g_ffn, w_o):
    B, S, D = x.shape
    tm = 512
    half = oa.shape[2]
    return pl.pallas_call(
        _outproj_kernel,
        grid=(B, S // tm),
        in_specs=[pl.BlockSpec((1, tm, half), lambda b, i: (b, i, 0)),
                  pl.BlockSpec((1, tm, half), lambda b, i: (b, i, 0)),
                  pl.BlockSpec((1, tm, D), lambda b, i: (b, i, 0)),
                  pl.BlockSpec((1, N_MOD, D), lambda b, i: (b, 0, 0)),
                  pl.BlockSpec((1, D), lambda b, i: (0, 0)),
                  pl.BlockSpec((2 * half, D), lambda b, i: (0, 0))],
        out_specs=[pl.BlockSpec((1, tm, D), lambda b, i: (b, i, 0)),
                   pl.BlockSpec((1, tm, D), lambda b, i: (b, i, 0))],
        out_shape=[jax.ShapeDtypeStruct((B, S, D), F32),
                   jax.ShapeDtypeStruct((B, S, D), BF16)],
        compiler_params=_cparams(("arbitrary", "arbitrary")),
        name="out_proj",
    )(oa, ob, x, mod, g_ffn.reshape(1, D), w_o)


def _ffn_kernel(h2_ref, x1_ref, mod_ref, wg_ref, wu_ref, cw_ref, cb_ref, wd_ref, gf_ref, o_ref,
                acc_ref, carry_ref, *, tm):
    i = pl.program_id(1)
    j = pl.program_id(2)
    nj = pl.num_programs(2)

    @pl.when(j == 0)
    def _():
        acc_ref[...] = jnp.zeros(acc_ref.shape, F32)

    h2 = h2_ref[0]
    g = jnp.dot(h2, wg_ref[...], preferred_element_type=F32)
    u = jnp.dot(h2, wu_ref[...], preferred_element_type=F32)

    prev = jnp.where(i > 0, carry_ref[j], 0.0)
    carry_ref[j] = g[tm - 8:, :]
    row = lax.broadcasted_iota(jnp.int32, g.shape, 0)
    g1 = jnp.where(row == 0, prev[7:8, :], pltpu.roll(g, 1, axis=0))
    g2 = jnp.where(row == 0, prev[6:7, :], jnp.where(row == 1, prev[7:8, :], pltpu.roll(g, 2, axis=0)))
    gc = cw_ref[0:1, :] * g2 + cw_ref[1:2, :] * g1 + cw_ref[2:3, :] * g + cb_ref[...]
    a = gc / (1.0 + jnp.exp(-gc)) * u
    acc_ref[...] += jnp.dot(a.astype(BF16), wd_ref[...], preferred_element_type=F32)

    @pl.when(j == nj - 1)
    def _():
        x2 = x1_ref[0] + mod_ref[0, 5:6, :] * acc_ref[...]
        ms = jnp.mean(x2 * x2, axis=-1, keepdims=True)
        o_ref[0] = x2 * lax.rsqrt(ms + EPS) * gf_ref[...]


def _conv_ffn(h2, x1, mod, w_gate, w_up, conv_w, conv_b, w_down, g_final):
    B, S, D = x1.shape
    F = w_gate.shape[1]
    tm, tf = 512, 512
    kern = functools.partial(_ffn_kernel, tm=tm)
    return pl.pallas_call(
        kern,
        grid=(B, S // tm, F // tf),
        in_specs=[pl.BlockSpec((1, tm, D), lambda b, i, j: (b, i, 0)),
                  pl.BlockSpec((1, tm, D), lambda b, i, j: (b, i, 0)),
                  pl.BlockSpec((1, N_MOD, D), lambda b, i, j: (b, 0, 0)),
                  pl.BlockSpec((D, tf), lambda b, i, j: (0, j)),
                  pl.BlockSpec((D, tf), lambda b, i, j: (0, j)),
                  pl.BlockSpec((3, tf), lambda b, i, j: (0, j)),
                  pl.BlockSpec((1, tf), lambda b, i, j: (0, j)),
                  pl.BlockSpec((tf, D), lambda b, i, j: (j, 0)),
                  pl.BlockSpec((1, D), lambda b, i, j: (0, 0))],
        out_specs=pl.BlockSpec((1, tm, D), lambda b, i, j: (b, i, 0)),
        out_shape=jax.ShapeDtypeStruct((B, S, D), F32),
        scratch_shapes=[pltpu.VMEM((tm, D), F32),
                        pltpu.VMEM((F // tf, 8, tf), F32)],
        compiler_params=_cparams(("arbitrary", "arbitrary", "arbitrary")),
        name="conv_ffn",
    )(h2, x1, mod, w_gate, w_up, conv_w, conv_b.reshape(1, F), w_down, g_final.reshape(1, D))


def _regroup_w_in(w_in):
    sizes = [1024, 256, 256, 1024, 64, 16, 1024, 256, 256]
    offs = [0]
    for s in sizes:
        offs.append(offs[-1] + s)
    w_t = w_in.T
    qa, ka, va, qi, ki, wi, qb, kb, vb = [w_t[offs[n]:offs[n + 1]] for n in range(9)]
    z64 = jnp.zeros_like(ki)
    big = jnp.concatenate([qa, qi, qb, ka, va, kb, vb, ki, z64, z64, ki], axis=0).astype(BF16)
    tail = jnp.concatenate([wi, jnp.zeros((128 - IDX_HEADS, w_in.shape[0]), w_in.dtype)], axis=0).astype(BF16)
    return big, tail


def kernel(x, c, w_ada, b_ada, g_mix, w_in, sinks, w_o, g_ffn, w_gate, w_up, conv_w, conv_b, w_down, g_final):
    assert w_ada.shape[0] == 1, "the final norm is fused into the (single) layer's FFN kernel"
    mod = _ada_mod(c, w_ada[0], b_ada[0])
    w_big, w_tail = _regroup_w_in(w_in[0])
    proj, tail, (wo_b, wg_b, wu_b, wd_b) = _in_proj(x, mod, g_mix[0], w_big, w_tail,
                                                    (w_o[0], w_gate[0], w_up[0], w_down[0]))
    oa = _dsa_attention(proj, tail)
    ob = _swa_attention(proj, sinks[0])
    x1, h2 = _out_proj(oa, ob, x, mod, g_ffn[0], wo_b)
    return _conv_ffn(h2, x1, mod, wg_b, wu_b, conv_w[0], conv_b[0], wd_b, g_final)
```
